```python
import math
import numpy as np
import jax
import jax.numpy as jnp
from jax import lax

D_MODEL = 2048
BATCH = 16
SEQ = 256
DEPTH = 4
DEC_BATCH = 8
DEC_SEQ = 2048
PAST_LEN = 512

GRID_W = 64
N_MIXERS = 4
N_DIFF = (DEPTH + 3) // 4
N_NA = (DEPTH + 2) // 4
N_MLA = (DEPTH + 1) // 4
N_SSD = DEPTH // 4
D_FF = 5632
Q_BLOCK = 128
ROPE_BASE = 10000.0
EPS = 1e-6

DIFF_HEADS = 8
DIFF_HEAD_DIM = D_MODEL // (2 * DIFF_HEADS)

NA_HEADS = 16
NA_HEAD_DIM = D_MODEL // NA_HEADS
NA_WIN_ROWS = 8
NA_WIN_COLS = 16
NA_QCOLS = 16
NA_KCOLS = 32

MLA_HEADS = 16
MLA_Q_LORA = 512
MLA_KV_LORA = 256
MLA_NOPE = 128
MLA_ROPE = 64
MLA_V = 128

SSD_D_INNER = 2 * D_MODEL
SSD_HEAD_DIM = 64
SSD_HEADS = SSD_D_INNER // SSD_HEAD_DIM
SSD_GROUPS = 8
SSD_HPG = SSD_HEADS // SSD_GROUPS
SSD_STATE = 128
SSD_CONV_W = 7
SSD_CHUNK = 128
SSD_CONV_CH = SSD_D_INNER + 2 * SSD_GROUPS * SSD_STATE
SSD_IN = SSD_D_INNER + SSD_CONV_CH + 2 * SSD_HEADS

kernel_name = 'bidir_hybrid_diffusion_step'


def _rms(x, w):
    xf = x.astype(jnp.float32)
    y = xf * lax.rsqrt(jnp.mean(xf * xf, axis=-1, keepdims=True) + EPS)
    return (y * w.astype(jnp.float32)).astype(x.dtype)


def _modulation(cond, w, b):
    m = jax.nn.silu(cond) @ w + b
    return m.reshape(cond.shape[0], 1, 3, 3, D_MODEL)


def _pre(x, mod, k, g):
    return _rms(x, g) * (1.0 + mod[:, :, k, 1]) + mod[:, :, k, 0]


def _swiglu(h, w_gu, w_down):
    g, u = jnp.split(h @ w_gu, 2, axis=-1)
    return (jax.nn.silu(g) * u) @ w_down


def _rope_axis(x, pos):
    h = x.shape[-1] // 2
    inv = ROPE_BASE ** (-jnp.arange(h, dtype=jnp.float32) / h)
    ang = pos.astype(jnp.float32)[:, None] * inv[None]
    shp = (1, pos.shape[0]) + (1,) * (x.ndim - 3) + (h,)
    cos, sin = jnp.cos(ang).reshape(shp), jnp.sin(ang).reshape(shp)
    xf = x.astype(jnp.float32)
    x1, x2 = xf[..., :h], xf[..., h:]
    return jnp.concatenate([x1 * cos - x2 * sin, x1 * sin + x2 * cos], axis=-1)


def _axial_rope(x):
    t = jnp.arange(x.shape[1])
    d = x.shape[-1] // 2
    out = jnp.concatenate([_rope_axis(x[..., :d], t // GRID_W),
                           _rope_axis(x[..., d:], t % GRID_W)], axis=-1)
    return out.astype(x.dtype)


def _to_blocks(q):
    b, s = q.shape[:2]
    return jnp.moveaxis(q.reshape((b, s // Q_BLOCK, Q_BLOCK) + q.shape[2:]), 1, 0)


def _from_blocks(o):
    o = jnp.moveaxis(o, 0, 1)
    return o.reshape((o.shape[0], -1) + o.shape[3:])


def _attention(q, k, v, scale):
    def block(qb):
        s = jnp.einsum('bqhd,bkhd->bhqk', qb, k).astype(jnp.float32) * scale
        p = jax.nn.softmax(s, axis=-1).astype(v.dtype)
        return jnp.einsum('bhqk,bkhd->bqhd', p, v)
    return _from_blocks(lax.map(block, _to_blocks(q)))


def _diff_attention(q, k, v, lam, scale):
    def block(qb):
        s = jnp.einsum('bqhmd,bkhmd->bhmqk', qb, k).astype(jnp.float32) * scale
        p = jax.nn.softmax(s, axis=-1)
        w = (p[:, :, 0] - lam * p[:, :, 1]).astype(v.dtype)
        return jnp.einsum('bhqk,bkhe->bqhe', w, v)
    return _from_blocks(lax.map(block, _to_blocks(q)))


def _diff_project(h, w_qkv):
    b, l, _ = h.shape
    q, k, v = jnp.split(h @ w_qkv, 3, axis=-1)
    return (q.reshape(b, l, DIFF_HEADS, 2, DIFF_HEAD_DIM),
            k.reshape(b, l, DIFF_HEADS, 2, DIFF_HEAD_DIM),
            v.reshape(b, l, DIFF_HEADS, 2 * DIFF_HEAD_DIM))


def _diff_lambda(lam_p, lam_init):
    lp = lam_p.astype(jnp.float32)
    return jnp.exp(jnp.sum(lp[0] * lp[1])) - jnp.exp(jnp.sum(lp[2] * lp[3])) + lam_init


def _diff_out(o, lam_init, subln_w, w_o):
    o = _rms(o, subln_w) * (1.0 - lam_init)
    return o.reshape(o.shape[0], o.shape[1], -1) @ w_o


def _na_project(h, w_qkv):
    b, l, _ = h.shape
    q, k, v = jnp.split(h @ w_qkv, 3, axis=-1)
    shp = (b, l, NA_HEADS, NA_HEAD_DIM)
    return q.reshape(shp), k.reshape(shp), v.reshape(shp)


def _na_latent(q, k, v, ck, cv, rpb, scale):
    b, L, H, dh = q.shape
    R = L // GRID_W
    kh = min(NA_WIN_ROWS, R)
    nqb = GRID_W // NA_QCOLS
    qcol = np.arange(GRID_W).reshape(nqb, NA_QCOLS)
    kstart = np.clip(np.arange(nqb) * NA_QCOLS - NA_WIN_COLS // 2, 0, GRID_W - NA_KCOLS)
    kcol = kstart[:, None] + np.arange(NA_KCOLS)[None]
    cstart = np.clip(qcol - NA_WIN_COLS // 2, 0, GRID_W - NA_WIN_COLS)
    kc3 = kcol[:, None, :]
    col_ok = (kc3 >= cstart[..., None]) & (kc3 < cstart[..., None] + NA_WIN_COLS)
    dc = np.clip(kc3 - qcol[..., None] + NA_WIN_COLS - 1, 0, 2 * NA_WIN_COLS - 2)
    rpb_c = rpb[:, :, dc]
    kg = k.reshape(b, R, GRID_W, H, dh)
    vg = v.reshape(b, R, GRID_W, H, dh)
    qg = jnp.moveaxis(q.reshape(b, R, nqb, NA_QCOLS, H, dh), 1, 0)
    nloc = kh * NA_KCOLS

    def row(args):
        r, qr = args
        rs = jnp.clip(r - kh // 2, 0, R - kh)
        kr = lax.dynamic_slice_in_dim(kg, rs, kh, axis=1)[:, :, kcol]
        vr = lax.dynamic_slice_in_dim(vg, rs, kh, axis=1)[:, :, kcol]
        ridx = rs + jnp.arange(kh) - r + NA_WIN_ROWS - 1
        bias = jnp.transpose(rpb_c[:, ridx], (0, 2, 3, 1, 4)).astype(jnp.float32)
        s_loc = jnp.einsum('bjqhd,brjkhd->bhjqrk', qr, kr).astype(jnp.float32) * scale + bias
        s_loc = jnp.where(col_ok[:, :, None, :], s_loc, -1e30)
        s_ctx = jnp.einsum('bjqhd,bphd->bhjqp', qr, ck).astype(jnp.float32) * scale
        s = jnp.concatenate([s_loc.reshape(b, H, nqb, NA_QCOLS, nloc), s_ctx], axis=-1)
        p = jax.nn.softmax(s, axis=-1).astype(v.dtype)
        p_loc = p[..., :nloc].reshape(b, H, nqb, NA_QCOLS, kh, NA_KCOLS)
        o = (jnp.einsum('bhjqrk,brjkhd->bjqhd', p_loc, vr)
             + jnp.einsum('bhjqp,bphd->bjqhd', p[..., nloc:], cv))
        return o.reshape(b, GRID_W, H, dh)

    o = lax.map(row, (jnp.arange(R), qg))
    return jnp.moveaxis(o, 0, 1).reshape(b, L, H, dh)


def _mla_queries(h, w_dq, q_norm, w_uq):
    b, l, _ = h.shape
    return (_rms(h @ w_dq, q_norm) @ w_uq).reshape(b, l, MLA_HEADS, MLA_NOPE + MLA_ROPE)


def _mla_compress(h, w_dkv, kv_norm):
    ckv = h @ w_dkv
    return _rms(ckv[..., :MLA_KV_LORA], kv_norm), ckv[..., MLA_KV_LORA:]


def _mla_expand(ckv, krope, w_ukv):
    b, l, _ = ckv.shape
    kv = (ckv @ w_ukv).reshape(b, l, MLA_HEADS, MLA_NOPE + MLA_V)
    kr = jnp.broadcast_to(krope[:, :, None].astype(kv.dtype), (b, l, MLA_HEADS, MLA_ROPE))
    return jnp.concatenate([kv[..., :MLA_NOPE], kr], axis=-1), kv[..., MLA_NOPE:]


def _dwconv_silu(u, w, bias):
    y = lax.conv_general_dilated(u, w[:, None, :].astype(u.dtype), window_strides=(1,),
                                 padding=[(SSD_CONV_W // 2, SSD_CONV_W // 2)],
                                 dimension_numbers=('NWC', 'WIO', 'NWC'),
                                 feature_group_count=u.shape[-1])
    return jax.nn.silu(y + bias)


def _ssd_scan(x, dt, A, Bm, Cm, h0):
    b, L, G, J, P = x.shape
    nc, Q = L // SSD_CHUNK, SSD_CHUNK
    xf = x.astype(jnp.float32).reshape(b, nc, Q, G, J, P)
    dt = dt.reshape(b, nc, Q, G, J)
    Bf = Bm.astype(jnp.float32).reshape(b, nc, Q, G, -1)
    Cf = Cm.astype(jnp.float32).reshape(b, nc, Q, G, -1)
    a_cum = jnp.cumsum(dt * A, axis=2)
    causal = jnp.tril(jnp.ones((Q, Q), bool))[None, None, :, :, None, None]
    seg = a_cum[:, :, :, None] - a_cum[:, :, None, :]
    lmat = jnp.exp(jnp.where(causal, seg, -jnp.inf))
    xdt = xf * dt[..., None]
    cb = jnp.einsum('bcqgn,bcsgn->bcqsg', Cf, Bf)
    y_diag = jnp.einsum('bcqsg,bcqsgj,bcsgjp->bcqgjp', cb, lmat, xdt)
    to_end = jnp.exp(a_cum[:, :, -1:] - a_cum)
    states = jnp.einsum('bcsgn,bcsgj,bcsgjp->bcgjpn', Bf, to_end, xdt)
    chunk_decay = jnp.exp(a_cum[:, :, -1])

    def step(h, inp):
        st, dcy = inp
        return h * dcy[..., None, None] + st, h

    h_last, h_in = lax.scan(step, h0, (jnp.moveaxis(states, 1, 0), jnp.moveaxis(chunk_decay, 1, 0)))
    h_in = jnp.moveaxis(h_in, 0, 1)
    y_off = jnp.einsum('bcqgn,bcgjpn,bcqgj->bcqgjp', Cf, h_in, jnp.exp(a_cum))
    return (y_diag + y_off).reshape(b, L, G, J, P), h_last


def _ssd_mix(h, h0, w_in, conv_w, conv_b, a_log, dt_bias, d_skip, norm_w, w_out):
    b, l, _ = h.shape
    proj = h @ w_in
    z = proj[..., :SSD_D_INNER]
    xbc = _dwconv_silu(proj[..., SSD_D_INNER:SSD_D_INNER + SSD_CONV_CH], conv_w, conv_b)
    gn = SSD_GROUPS * SSD_STATE
    x = xbc[..., :SSD_D_INNER].reshape(b, l, SSD_GROUPS, SSD_HPG, SSD_HEAD_DIM)
    Bm = xbc[..., SSD_D_INNER:SSD_D_INNER + gn].reshape(b, l, SSD_GROUPS, SSD_STATE)
    Cm = xbc[..., SSD_D_INNER + gn:].reshape(b, l, SSD_GROUPS, SSD_STATE)
    dt = proj[..., SSD_D_INNER + SSD_CONV_CH:].reshape(b, l, 2, SSD_HEADS)
    dt = jax.nn.softplus(dt.astype(jnp.float32) + dt_bias.astype(jnp.float32))
    dt = dt.reshape(b, l, 2, SSD_GROUPS, SSD_HPG)
    A = -jnp.exp(a_log.astype(jnp.float32)).reshape(2, SSD_GROUPS, SSD_HPG)
    h0 = h0.astype(jnp.float32).reshape(b, 2, SSD_GROUPS, SSD_HPG, SSD_HEAD_DIM, SSD_STATE)
    flip = lambda t: jnp.flip(t, axis=1)
    y_f, h_f = _ssd_scan(x, dt[:, :, 0], A[0], Bm, Cm, h0[:, 0])
    y_b, h_b = _ssd_scan(flip(x), flip(dt[:, :, 1]), A[1], flip(Bm), flip(Cm), h0[:, 1])
    dsk = d_skip.astype(jnp.float32).reshape(SSD_GROUPS, SSD_HPG)[..., None]
    y = y_f + flip(y_b) + dsk * x.astype(jnp.float32)
    y = y.reshape(b, l, SSD_D_INNER).astype(h.dtype)
    out = _rms(y * jax.nn.silu(z), norm_w) @ w_out
    h_T = jnp.stack([h_f, h_b], axis=1).reshape(b, 2, SSD_HEADS, SSD_HEAD_DIM, SSD_STATE)
    return out, h_T


def setup_inputs(seed: int = 0) -> dict:
    key = jax.random.key(seed)
    ks = iter(jax.random.split(key, 64))
    f32 = jnp.float32
    D = D_MODEL

    def nrm(shape, scale=1.0):
        return scale * jax.random.normal(next(ks), shape, f32)

    def gain(shape):
        return 1.0 + nrm(shape, 0.05)

    a_log = jnp.log(jax.random.uniform(next(ks), (N_SSD, 2, SSD_HEADS), f32, 1.0, 16.0))
    dt0 = jnp.exp(jax.random.uniform(next(ks), (N_SSD, 2, SSD_HEADS), f32,
                                     math.log(1e-3), math.log(1e-1)))
    dt_bias = dt0 + jnp.log(-jnp.expm1(-dt0))
    return {
        'x_prompt': nrm((BATCH, SEQ, D)),
        'x_sample': nrm((DEC_BATCH, DEC_SEQ, D)),
        'cache_diff_k': nrm((DEC_BATCH, N_DIFF, PAST_LEN, DIFF_HEADS, 2, DIFF_HEAD_DIM)),
        'cache_diff_v': nrm((DEC_BATCH, N_DIFF, PAST_LEN, DIFF_HEADS, 2 * DIFF_HEAD_DIM)),
        'cache_na_k': nrm((DEC_BATCH, N_NA, PAST_LEN, NA_HEADS, NA_HEAD_DIM)),
        'cache_na_v': nrm((DEC_BATCH, N_NA, PAST_LEN, NA_HEADS, NA_HEAD_DIM)),
        'cache_mla_ckv': nrm((DEC_BATCH, N_MLA, PAST_LEN, MLA_KV_LORA)),
        'cache_mla_krope': nrm((DEC_BATCH, N_MLA, PAST_LEN, MLA_ROPE)),
        'state_ssd': nrm((DEC_BATCH, N_SSD, 2, SSD_HEADS, SSD_HEAD_DIM, SSD_STATE), 0.1),
        'c': nrm((DEC_BATCH, D)),
        'c_ctx': nrm((D,)),
        'ada_w': nrm((DEPTH, D, 9 * D), 0.5 * D ** -0.5),
        'ada_b': nrm((DEPTH, 9 * D), 0.02),
        'norm_w': gain((DEPTH, 3, D)),
        'ffn_w_gu': nrm((DEPTH, 2, D, 2 * D_FF), D ** -0.5),
        'ffn_w_down': nrm((DEPTH, 2, D_FF, D), D_FF ** -0.5),
        'final_norm_w': gain((D,)),
        'diff_w_qkv': nrm((N_DIFF, D, 3 * D), D ** -0.5),
        'diff_w_o': nrm((N_DIFF, D, D), D ** -0.5),
        'diff_lambda': nrm((N_DIFF, 4, DIFF_HEAD_DIM), 0.1),
        'diff_subln_w': gain((N_DIFF, 2 * DIFF_HEAD_DIM)),
        'na_w_qkv': nrm((N_NA, D, 3 * D), D ** -0.5),
        'na_w_o': nrm((N_NA, D, D), D ** -0.5),
        'na_rpb': nrm((N_NA, NA_HEADS, 2 * NA_WIN_ROWS - 1, 2 * NA_WIN_COLS - 1), 0.02),
        'mla_w_dq': nrm((N_MLA, D, MLA_Q_LORA), D ** -0.5),
        'mla_q_norm': gain((N_MLA, MLA_Q_LORA)),
        'mla_w_uq': nrm((N_MLA, MLA_Q_LORA, MLA_HEADS * (MLA_NOPE + MLA_ROPE)), MLA_Q_LORA ** -0.5),
        'mla_w_dkv': nrm((N_MLA, D, MLA_KV_LORA + MLA_ROPE), D ** -0.5),
        'mla_kv_norm': gain((N_MLA, MLA_KV_LORA)),
        'mla_w_ukv': nrm((N_MLA, MLA_KV_LORA, MLA_HEADS * (MLA_NOPE + MLA_V)), MLA_KV_LORA ** -0.5),
        'mla_w_o': nrm((N_MLA, MLA_HEADS * MLA_V, D), (MLA_HEADS * MLA_V) ** -0.5),
        'ssd_w_in': nrm((N_SSD, D, SSD_IN), D ** -0.5),
        'ssd_conv_w': nrm((N_SSD, SSD_CONV_W, SSD_CONV_CH), SSD_CONV_W ** -0.5),
        'ssd_conv_b': nrm((N_SSD, SSD_CONV_CH), 0.02),
        'ssd_a_log': a_log,
        'ssd_dt_bias': dt_bias,
        'ssd_d': gain((N_SSD, SSD_HEADS)),
        'ssd_norm_w': gain((N_SSD, SSD_D_INNER)),
        'ssd_w_out': nrm((N_SSD, SSD_D_INNER, D), SSD_D_INNER ** -0.5),
    }


def reference(x_prompt, x_sample, cache_diff_k, cache_diff_v, cache_na_k, cache_na_v,
              cache_mla_ckv, cache_mla_krope, state_ssd, c, c_ctx,
              ada_w, ada_b, norm_w, ffn_w_gu, ffn_w_down, final_norm_w,
              diff_w_qkv, diff_w_o, diff_lambda, diff_subln_w,
              na_w_qkv, na_w_o, na_rpb,
              mla_w_dq, mla_q_norm, mla_w_uq, mla_w_dkv, mla_kv_norm, mla_w_ukv, mla_w_o,
              ssd_w_in, ssd_conv_w, ssd_conv_b, ssd_a_log, ssd_dt_bias, ssd_d, ssd_norm_w, ssd_w_out):
    xp, xs = x_prompt, x_sample
    bp, bs = xp.shape[0], xs.shape[0]
    new_dk, new_dv, new_nk, new_nv, new_ckv, new_kr, new_h = [], [], [], [], [], [], []
    for i in range(DEPTH):
        kind, j = i % N_MIXERS, i // N_MIXERS
        mp = _modulation(c_ctx[None], ada_w[i], ada_b[i])
        ms = _modulation(c, ada_w[i], ada_b[i])
        xp = xp + 0.5 * mp[:, :, 0, 2] * _swiglu(_pre(xp, mp, 0, norm_w[i, 0]), ffn_w_gu[i, 0], ffn_w_down[i, 0])
        xs = xs + 0.5 * ms[:, :, 0, 2] * _swiglu(_pre(xs, ms, 0, norm_w[i, 0]), ffn_w_gu[i, 0], ffn_w_down[i, 0])
        hp = _pre(xp, mp, 1, norm_w[i, 1])
        hs = _pre(xs, ms, 1, norm_w[i, 1])
        if kind == 0:
            lam_init = 0.8 - 0.6 * math.exp(-0.3 * i)
            lam = _diff_lambda(diff_lambda[j], lam_init)
            scale = DIFF_HEAD_DIM ** -0.5
            q, k, v = _diff_project(hp, diff_w_qkv[j])
            o_p = _diff_out(_diff_attention(q, k, v, lam, scale), lam_init, diff_subln_w[j], diff_w_o[j])
            new_dk.append(k)
            new_dv.append(v)
            q, k, v = _diff_project(hs, diff_w_qkv[j])
            k_all = jnp.concatenate([_axial_rope(k), cache_diff_k[:, j].astype(k.dtype)], axis=1)
            v_all = jnp.concatenate([v, cache_diff_v[:, j].astype(v.dtype)], axis=1)
            o_s = _diff_out(_diff_attention(_axial_rope(q), k_all, v_all, lam, scale),
                            lam_init, diff_subln_w[j], diff_w_o[j])
        elif kind == 1:
            scale = NA_HEAD_DIM ** -0.5
            q, k, v = _na_project(hp, na_w_qkv[j])
            o_p = _attention(q, k, v, scale).reshape(bp, -1, D_MODEL) @ na_w_o[j]
            new_nk.append(k)
            new_nv.append(v)
            q, k, v = _na_project(hs, na_w_qkv[j])
            o = _na_latent(q, k, v, cache_na_k[:, j].astype(k.dtype), cache_na_v[:, j].astype(v.dtype),
                           na_rpb[j], scale)
            o_s = o.reshape(bs, -1, D_MODEL) @ na_w_o[j]
        elif kind == 2:
            scale = (MLA_NOPE + MLA_ROPE) ** -0.5
            q = _mla_queries(hp, mla_w_dq[j], mla_q_norm[j], mla_w_uq[j])
            ckv, kr = _mla_compress(hp, mla_w_dkv[j], mla_kv_norm[j])
            k, v = _mla_expand(ckv, kr, mla_w_ukv[j])
            o_p = _attention(q, k, v, scale).reshape(bp, -1, MLA_HEADS * MLA_V) @ mla_w_o[j]
            new_ckv.append(ckv)
            new_kr.append(kr)
            q = _mla_queries(hs, mla_w_dq[j], mla_q_norm[j], mla_w_uq[j])
            q = jnp.concatenate([q[..., :MLA_NOPE], _axial_rope(q[..., MLA_NOPE:])], axis=-1)
            ckv, kr = _mla_compress(hs, mla_w_dkv[j], mla_kv_norm[j])
            kr = _axial_rope(kr[:, :, None])[:, :, 0]
            k, v = _mla_expand(ckv, kr, mla_w_ukv[j])
            k_c, v_c = _mla_expand(cache_mla_ckv[:, j].astype(ckv.dtype), cache_mla_krope[:, j], mla_w_ukv[j])
            o = _attention(q, jnp.concatenate([k, k_c], axis=1), jnp.concatenate([v, v_c], axis=1), scale)
            o_s = o.reshape(bs, -1, MLA_HEADS * MLA_V) @ mla_w_o[j]
        else:
            h0 = jnp.zeros((bp, 2, SSD_HEADS, SSD_HEAD_DIM, SSD_STATE), jnp.float32)
            o_p, h_T = _ssd_mix(hp, h0, ssd_w_in[j], ssd_conv_w[j], ssd_conv_b[j], ssd_a_log[j],
                                ssd_dt_bias[j], ssd_d[j], ssd_norm_w[j], ssd_w_out[j])
            new_h.append(h_T.astype(xp.dtype))
            o_s, _ = _ssd_mix(hs, state_ssd[:, j], ssd_w_in[j], ssd_conv_w[j], ssd_conv_b[j], ssd_a_log[j],
                              ssd_dt_bias[j], ssd_d[j], ssd_norm_w[j], ssd_w_out[j])
        xp = xp + mp[:, :, 1, 2] * o_p
        xs = xs + ms[:, :, 1, 2] * o_s
        xp = xp + 0.5 * mp[:, :, 2, 2] * _swiglu(_pre(xp, mp, 2, norm_w[i, 2]), ffn_w_gu[i, 1], ffn_w_down[i, 1])
        xs = xs + 0.5 * ms[:, :, 2, 2] * _swiglu(_pre(xs, ms, 2, norm_w[i, 2]), ffn_w_gu[i, 1], ffn_w_down[i, 1])
    y_prompt = _rms(xp, final_norm_w)
    y_sample = _rms(xs, final_norm_w)
    new_diff_k = jnp.stack(new_dk, axis=1)
    new_diff_v = jnp.stack(new_dv, axis=1)
    new_na_k = jnp.stack(new_nk, axis=1)
    new_na_v = jnp.stack(new_nv, axis=1)
    new_mla_ckv = jnp.stack(new_ckv, axis=1)
    new_mla_krope = jnp.stack(new_kr, axis=1)
    new_state_ssd = jnp.stack(new_h, axis=1)
    return (y_prompt, y_sample, new_diff_k, new_diff_v, new_na_k, new_na_v, new_mla_ckv, new_mla_krope, new_state_ssd)
```

```python
import functools
import math

import numpy as np
import jax
import jax.numpy as jnp
from jax import lax
from jax.experimental import pallas as pl
from jax.experimental.pallas import tpu as pltpu

f32 = jnp.float32
bf16 = jnp.bfloat16

D = 2048
BP, SP = 16, 256
BS, SS = 8, 2048
PAST = 512
GRID_W = 64
GRID_R = SS // GRID_W
DEPTH = 4
D_FF = 5632
EPS = 1e-6
ROPE_BASE = 10000.0
T_P = BP * SP
T_S = BS * SS
T = T_P + T_S
N_COND = 1 + BS
COND_PAD = 16

DIFF_HEADS, DIFF_HD = 8, 128
NA_HEADS, NA_HD = 16, 128
NA_WIN_ROWS, NA_WIN_COLS = 8, 16
NA_QROWS = 4
NA_KROWS = 12
MLA_HEADS, MLA_Q_LORA, MLA_KV_LORA = 16, 512, 256
MLA_NOPE, MLA_ROPE, MLA_V = 128, 64, 128
MLA_QW = 256
SSD_DI = 2 * D
SSD_P, SSD_HEADS, SSD_G, SSD_HPG, SSD_N = 64, 64, 8, 8, 128
SSD_CONV_W, SSD_Q = 7, 128
SSD_CONV_CH = SSD_DI + 2 * SSD_G * SSD_N
SSD_IN = SSD_DI + SSD_CONV_CH + 2 * SSD_HEADS
SSD_GW = SSD_HPG * SSD_P

V7X_LANES = 128
V7X_VMEM_LIMIT = 60000 * 1024

TM = 512


def _cparams(sem):
    return pltpu.CompilerParams(dimension_semantics=sem, vmem_limit_bytes=V7X_VMEM_LIMIT)


def _mod_row(i, tm):
    start = i * tm
    return jnp.where(start < T_P, 0, 1 + (start - T_P) // SS)


def _silu(x):
    return x / (1.0 + jnp.exp(-x))


def _dot(a, b):
    return jnp.dot(a, b, preferred_element_type=f32)


def _dot_nt(a, b):
    return lax.dot_general(a, b, (((1,), (1,)), ((), ())), preferred_element_type=f32)


def _dot_tn(a, b):
    return lax.dot_general(a, b, (((0,), (0,)), ((), ())), preferred_element_type=f32)


def _dot_exact(a, b):
    return jnp.dot(a, b, preferred_element_type=f32, precision=lax.Precision.HIGHEST)


def _mod_kernel(c_ref, w_ref, b_ref, o_ref):
    s = _silu(c_ref[...])
    o_ref[0] = jnp.dot(s, w_ref[0], preferred_element_type=f32, precision=lax.Precision.HIGHEST) + b_ref[0]


def _modulation(cond, ada_w, ada_b):
    n = ada_w.shape[-1]
    tn = 1024
    return pl.pallas_call(
        _mod_kernel,
        grid=(DEPTH, n // tn),
        in_specs=[pl.BlockSpec((COND_PAD, D), lambda l, j: (0, 0)),
                  pl.BlockSpec((1, D, tn), lambda l, j: (l, 0, j)),
                  pl.BlockSpec((1, 1, tn), lambda l, j: (l, 0, j))],
        out_specs=pl.BlockSpec((1, COND_PAD, tn), lambda l, j: (l, 0, j)),
        out_shape=jax.ShapeDtypeStruct((DEPTH, COND_PAD, n), f32),
        compiler_params=_cparams(("arbitrary", "arbitrary")),
        name="modulation",
    )(cond, ada_w, ada_b.reshape(DEPTH, 1, n))


def _rope_apply(acc, cos, sin, shift):
    w = acc.shape[1]
    reps = w // cos.shape[1]
    if reps > 1:
        cos = jnp.concatenate([cos] * reps, axis=1)
        sin = jnp.concatenate([sin] * reps, axis=1)
    lane = lax.broadcasted_iota(jnp.int32, acc.shape, 1)
    first = (lane % (2 * shift)) < shift
    rot = jnp.where(first, pltpu.roll(acc, w - shift, axis=1), pltpu.roll(acc, shift, axis=1))
    return acc * cos + rot * sin


def _fused_matmul_kernel(*refs, pre, epi, rope, want_pre_out, res_scale):
    refs = list(refs)
    a_ref = refs.pop(0)
    if pre == "mod":
        nw_ref, sh_ref, sc_ref = refs.pop(0), refs.pop(0), refs.pop(0)
    elif pre == "rms":
        nw_ref = refs.pop(0)
    elif pre == "gated_rms":
        z_ref, nw_ref = refs.pop(0), refs.pop(0)
    w_ref = refs.pop(0)
    if epi == "rope":
        cos_ref, sin_ref = refs.pop(0), refs.pop(0)
    elif epi == "residual":
        res_ref, gate_ref = refs.pop(0), refs.pop(0)
    o_ref = refs.pop(0)
    if want_pre_out:
        po_ref = refs.pop(0)
    h_ref = refs.pop(0)
    j = pl.program_id(1)

    @pl.when(j == 0)
    def _():
        a = a_ref[...].astype(f32)
        if pre == "gated_rms":
            a = a * _silu(z_ref[...])
        if pre != "none":
            a = a * lax.rsqrt(jnp.mean(a * a, axis=-1, keepdims=True) + EPS) * nw_ref[...]
        if pre == "mod":
            a = a * (1.0 + sc_ref[0]) + sh_ref[0]
        if want_pre_out:
            po_ref[...] = a
        h_ref[...] = a.astype(bf16)

    acc = _dot(h_ref[...], w_ref[...])
    if epi == "rope":
        n_tiles, lo, hi, shift = rope

        @pl.when(j < n_tiles)
        def _():
            if lo == 0 and hi == acc.shape[1]:
                o_ref[...] = _rope_apply(acc, cos_ref[...], sin_ref[...], shift)
            else:
                mid = _rope_apply(acc[:, lo:hi], cos_ref[...], sin_ref[...], shift)
                o_ref[...] = jnp.concatenate([acc[:, :lo], mid] + ([acc[:, hi:]] if hi < acc.shape[1] else []), axis=1)

        @pl.when(j >= n_tiles)
        def _():
            o_ref[...] = acc
    elif epi == "residual":
        o_ref[...] = res_ref[...] + (res_scale * gate_ref[0]) * acc
    else:
        o_ref[...] = acc.astype(o_ref.dtype)


def _fused_matmul(a, w, *, k_blk=0, pre="none", pre_args=(), epi="none", epi_args=(), rope=None,
                  want_pre_out=False, res_scale=1.0, tm=TM, tn=512, name="fused_matmul"):
    rows = a.shape[0]
    k, n = w.shape
    assert rows % tm == 0 and n % tn == 0
    grid = (rows // tm, n // tn)
    row_k = lambda i, j: (i, k_blk)
    vec_k = pl.BlockSpec((1, k), lambda i, j: (0, 0))
    mod_k = pl.BlockSpec((1, 1, k), lambda i, j: (_mod_row(i, tm), 0, 0))
    ins, specs = [a], [pl.BlockSpec((tm, k), row_k)]
    if pre == "mod":
        nw, sh, sc = pre_args
        ins += [nw.reshape(1, k), sh, sc]
        specs += [vec_k, mod_k, mod_k]
    elif pre == "rms":
        ins += [pre_args[0].reshape(1, k)]
        specs += [vec_k]
    elif pre == "gated_rms":
        z, nw = pre_args
        ins += [z, nw.reshape(1, k)]
        specs += [pl.BlockSpec((tm, k), lambda i, j: (i, 0)), vec_k]
    ins.append(w)
    specs.append(pl.BlockSpec((k, tn), lambda i, j: (0, j)))
    if epi == "rope":
        cos, sin = epi_args
        p = cos.shape[1]
        ins += [cos, sin]
        specs += [pl.BlockSpec((tm, p), lambda i, j: (i, 0))] * 2
    elif epi == "residual":
        res, gate = epi_args
        ins += [res, gate]
        specs += [pl.BlockSpec((tm, tn), lambda i, j: (i, j)),
                  pl.BlockSpec((1, 1, tn), lambda i, j: (_mod_row(i, tm), 0, j))]
    out_shape = [jax.ShapeDtypeStruct((rows, n), f32)]
    out_specs = [pl.BlockSpec((tm, tn), lambda i, j: (i, j))]
    if want_pre_out:
        out_shape.append(jax.ShapeDtypeStruct((rows, k), f32))
        out_specs.append(pl.BlockSpec((tm, k), lambda i, j: (i, 0)))
    outs = pl.pallas_call(
        functools.partial(_fused_matmul_kernel, pre=pre, epi=epi, rope=rope, want_pre_out=want_pre_out,
                          res_scale=res_scale),
        grid=grid, in_specs=specs, out_specs=out_specs, out_shape=out_shape,
        scratch_shapes=[pltpu.VMEM((tm, k), bf16)],
        compiler_params=_cparams(("parallel", "arbitrary")),
        name=name,
    )(*ins)
    return outs if want_pre_out else outs[0]


def _ffn_kernel(x_ref, nw_ref, sh_ref, sc_ref, gate_ref, wg_ref, wu_ref, wd_ref, fw_ref, o_ref, h_ref, acc_ref, *,
                final_norm):
    f = pl.program_id(1)

    @pl.when(f == 0)
    def _():
        x = x_ref[...]
        h = x * lax.rsqrt(jnp.mean(x * x, axis=-1, keepdims=True) + EPS) * nw_ref[...]
        h_ref[...] = (h * (1.0 + sc_ref[0]) + sh_ref[0]).astype(bf16)
        acc_ref[...] = jnp.zeros_like(acc_ref)

    h = h_ref[...]
    g = _dot(h, wg_ref[...])
    u = _dot(h, wu_ref[...])
    acc_ref[...] += _dot((_silu(g) * u).astype(bf16), wd_ref[...])

    @pl.when(f == pl.num_programs(1) - 1)
    def _():
        y = x_ref[...] + (0.5 * gate_ref[0]) * acc_ref[...]
        if final_norm:
            y = y * lax.rsqrt(jnp.mean(y * y, axis=-1, keepdims=True) + EPS) * fw_ref[...]
        o_ref[...] = y


def _ffn(x, nw, sh, sc, gate, wgu, wd, final_w, *, final_norm, tm=TM, tf=512):
    nf = D_FF // tf
    vec = pl.BlockSpec((1, D), lambda i, f: (0, 0))
    mod = pl.BlockSpec((1, 1, D), lambda i, f: (_mod_row(i, tm), 0, 0))
    return pl.pallas_call(
        functools.partial(_ffn_kernel, final_norm=final_norm),
        grid=(T // tm, nf),
        in_specs=[pl.BlockSpec((tm, D), lambda i, f: (i, 0)), vec, mod, mod, mod,
                  pl.BlockSpec((D, tf), lambda i, f: (0, f)),
                  pl.BlockSpec((D, tf), lambda i, f: (0, f + nf)),
                  pl.BlockSpec((tf, D), lambda i, f: (f, 0)),
                  vec],
        out_specs=pl.BlockSpec((tm, D), lambda i, f: (i, 0)),
        out_shape=jax.ShapeDtypeStruct((T, D), f32),
        scratch_shapes=[pltpu.VMEM((tm, D), bf16), pltpu.VMEM((tm, D), f32)],
        compiler_params=_cparams(("parallel", "arbitrary")),
        name="ffn",
    )(x, nw.reshape(1, D), sh, sc, gate, wgu, wgu, wd, final_w.reshape(1, D))


def _softmax_segments(segs):
    m = functools.reduce(jnp.maximum, [jnp.max(s, axis=-1, keepdims=True) for s in segs])
    es = [jnp.exp(s - m) for s in segs]
    l = functools.reduce(jnp.add, [jnp.sum(e, axis=-1, keepdims=True) for e in es])
    inv = 1.0 / l
    return [e * inv for e in es]


def _diff_attn_kernel(*refs, scale, lam_init, has_cache):
    if has_cache:
        lam_ref, sub_ref, q_ref, k_ref, v_ref, kc_ref, vc_ref, o_ref = refs
    else:
        lam_ref, sub_ref, q_ref, k_ref, v_ref, o_ref = refs
    lp = lam_ref[...]
    lam = (jnp.exp(jnp.sum(lp[0:1] * lp[1:2], axis=-1, keepdims=True))
           - jnp.exp(jnp.sum(lp[2:3] * lp[3:4], axis=-1, keepdims=True)) + lam_init)
    q = q_ref[...]
    ks = [k_ref[...]] + ([kc_ref[0]] if has_cache else [])
    vs = [v_ref[...]] + ([vc_ref[0]] if has_cache else [])
    probs = []
    for m in range(2):
        sl = slice(m * DIFF_HD, (m + 1) * DIFF_HD)
        qm = q[:, sl].astype(bf16)
        probs.append(_softmax_segments([_dot_nt(qm, kk[:, sl].astype(bf16)) * scale for kk in ks]))
    o = None
    for p1, p2, v in zip(probs[0], probs[1], vs):
        part = _dot((p1 - lam * p2).astype(bf16), v.astype(bf16))
        o = part if o is None else o + part
    o = o * lax.rsqrt(jnp.mean(o * o, axis=-1, keepdims=True) + EPS) * sub_ref[...]
    o_ref[...] = o * (1.0 - lam_init)


def _diff_attention(qkv, lam_p, subln, cache_k, cache_v, *, lam_init, latent, tq=256):
    hw = 2 * DIFF_HD
    scale = DIFF_HD ** -0.5
    if latent:
        nb, seq, row0 = BS, SS, T_P
    else:
        nb, seq, row0 = BP, SP, 0
    nq = seq // tq
    small = [pl.BlockSpec((4, DIFF_HD), lambda b, h, i: (0, 0)), pl.BlockSpec((1, hw), lambda b, h, i: (0, 0))]
    specs = small + [
        pl.BlockSpec((tq, hw), lambda b, h, i: ((row0 + b * seq) // tq + i, h)),
        pl.BlockSpec((seq, hw), lambda b, h, i: (row0 // seq + b, DIFF_HEADS + h)),
        pl.BlockSpec((seq, hw), lambda b, h, i: (row0 // seq + b, 2 * DIFF_HEADS + h))]
    ins = [lam_p, subln.reshape(1, hw), qkv, qkv, qkv]
    if latent:
        specs += [pl.BlockSpec((1, PAST, hw), lambda b, h, i: (b, 0, h))] * 2
        ins += [cache_k, cache_v]
    return pl.pallas_call(
        functools.partial(_diff_attn_kernel, scale=scale, lam_init=lam_init, has_cache=latent),
        grid=(nb, DIFF_HEADS, nq), in_specs=specs,
        out_specs=pl.BlockSpec((tq, hw), lambda b, h, i: (b * nq + i, h)),
        out_shape=jax.ShapeDtypeStruct((nb * seq, D), f32),
        compiler_params=_cparams(("parallel", "parallel", "arbitrary")),
        name="diff_attn_latent" if latent else "diff_attn_context",
    )(*ins)


def _attn_kernel(q_ref, k_ref, v_ref, o_ref, *, scale):
    s = _dot_nt(q_ref[...].astype(bf16), k_ref[...].astype(bf16)) * scale
    (p,) = _softmax_segments([s])
    o_ref[...] = _dot(p.astype(bf16), v_ref[...].astype(bf16))


def _context_attention(qkv, *, heads, hd, scale):
    return pl.pallas_call(
        functools.partial(_attn_kernel, scale=scale),
        grid=(BP, heads),
        in_specs=[pl.BlockSpec((SP, hd), lambda b, h: (b, h)),
                  pl.BlockSpec((SP, hd), lambda b, h: (b, heads + h)),
                  pl.BlockSpec((SP, hd), lambda b, h: (b, 2 * heads + h))],
        out_specs=pl.BlockSpec((SP, hd), lambda b, h: (b, h)),
        out_shape=jax.ShapeDtypeStruct((T_P, heads * hd), f32),
        compiler_params=_cparams(("parallel", "parallel")),
        name="context_attn",
    )(qkv, qkv, qkv)


def _mla_attn_kernel(*refs, scale, has_cache):
    if has_cache:
        q_ref, kn_ref, kr_ref, v_ref, knc_ref, krc_ref, vc_ref, o_ref = refs
    else:
        q_ref, kn_ref, kr_ref, v_ref, o_ref = refs
    q = q_ref[...].astype(bf16)
    ks = [jnp.concatenate([kn_ref[...], kr_ref[...]], axis=1)]
    vs = [v_ref[...]]
    if has_cache:
        ks.append(jnp.concatenate([knc_ref[...], krc_ref[...]], axis=1))
        vs.append(vc_ref[...])
    ps = _softmax_segments([_dot_nt(q, k.astype(bf16)) * scale for k in ks])
    o = None
    for p, v in zip(ps, vs):
        part = _dot(p.astype(bf16), v.astype(bf16))
        o = part if o is None else o + part
    o_ref[...] = o


def _mla_attention(q, kvx, lat, kvc, krc, *, latent, tq=256):
    scale = (MLA_NOPE + MLA_ROPE) ** -0.5
    kr_blk = (MLA_Q_LORA + MLA_KV_LORA) // V7X_LANES
    if latent:
        nb, seq, row0 = BS, SS, T_P
    else:
        nb, seq, row0 = BP, SP, 0
    nq = seq // tq
    specs = [pl.BlockSpec((tq, MLA_QW), lambda b, h, i: ((row0 + b * seq) // tq + i, h)),
             pl.BlockSpec((seq, MLA_NOPE), lambda b, h, i: (row0 // seq + b, 2 * h)),
             pl.BlockSpec((seq, V7X_LANES), lambda b, h, i: (row0 // seq + b, kr_blk)),
             pl.BlockSpec((seq, MLA_V), lambda b, h, i: (row0 // seq + b, 2 * h + 1))]
    ins = [q, kvx, lat, kvx]
    if latent:
        specs += [pl.BlockSpec((PAST, MLA_NOPE), lambda b, h, i: (b, 2 * h)),
                  pl.BlockSpec((PAST, V7X_LANES), lambda b, h, i: (b, 0)),
                  pl.BlockSpec((PAST, MLA_V), lambda b, h, i: (b, 2 * h + 1))]
        ins += [kvc, krc, kvc]
    return pl.pallas_call(
        functools.partial(_mla_attn_kernel, scale=scale, has_cache=latent),
        grid=(nb, MLA_HEADS, nq), in_specs=specs,
        out_specs=pl.BlockSpec((tq, MLA_V), lambda b, h, i: (b * nq + i, h)),
        out_shape=jax.ShapeDtypeStruct((nb * seq, MLA_HEADS * MLA_V), f32),
        compiler_params=_cparams(("parallel", "parallel", "arbitrary")),
        name="mla_attn_latent" if latent else "mla_attn_context",
    )(*ins)


def _na_block_plan():
    plan = []
    for r0 in range(0, GRID_R, NA_QROWS):
        kb = min(max(r0 - NA_WIN_ROWS // 2, 0), GRID_R - NA_KROWS)
        var = 0 if r0 == 0 else (2 if r0 == GRID_R - NA_QROWS else 1)
        plan.append((r0, kb, var))
    return plan


def _na_bias_tables(rpb):
    plan = _na_block_plan()
    reps = {var: (r0, kb) for r0, kb, var in plan}
    i = np.arange(NA_QROWS)[:, None, None, None]
    c = np.arange(GRID_W)[None, :, None, None]
    kk = np.arange(NA_KROWS)[None, None, :, None]
    kc = np.arange(GRID_W)[None, None, None, :]
    tabs = []
    for var in range(3):
        r0, kb = reps[var]
        r = r0 + i
        rs = np.clip(r - NA_WIN_ROWS // 2, 0, GRID_R - NA_WIN_ROWS)
        kr = kb + kk
        row_ok = (kr >= rs) & (kr < rs + NA_WIN_ROWS)
        cstart = np.clip(c - NA_WIN_COLS // 2, 0, GRID_W - NA_WIN_COLS)
        col_ok = (kc >= cstart) & (kc < cstart + NA_WIN_COLS)
        ridx = np.clip(kr - r + NA_WIN_ROWS - 1, 0, 2 * NA_WIN_ROWS - 2)
        dc = np.clip(kc - c + NA_WIN_COLS - 1, 0, 2 * NA_WIN_COLS - 2)
        ok = np.broadcast_to(row_ok & col_ok, (NA_QROWS, GRID_W, NA_KROWS, GRID_W))
        ridx = np.broadcast_to(ridx, ok.shape)
        dc = np.broadcast_to(dc, ok.shape)
        tab = jnp.where(ok[None], rpb[:, ridx, dc], -1e30)
        tabs.append(tab.reshape(NA_HEADS, NA_QROWS * GRID_W, NA_KROWS * GRID_W))
    return jnp.stack(tabs).astype(f32)


def _na_kernel(q_ref, k_ref, v_ref, ck_ref, cv_ref, bias_ref, o_ref, *, scale):
    ck = ck_ref[0].astype(bf16)
    cv = cv_ref[0].astype(bf16)
    nq = NA_QROWS * GRID_W
    nk = NA_KROWS * GRID_W
    for r0, kb, var in _na_block_plan():
        q = q_ref[r0 * GRID_W:r0 * GRID_W + nq, :].astype(bf16)
        k = k_ref[kb * GRID_W:kb * GRID_W + nk, :].astype(bf16)
        v = v_ref[kb * GRID_W:kb * GRID_W + nk, :].astype(bf16)
        s_loc = _dot_nt(q, k) * scale + bias_ref[var, 0]
        s_ctx = _dot_nt(q, ck) * scale
        p_loc, p_ctx = _softmax_segments([s_loc, s_ctx])
        o_ref[r0 * GRID_W:r0 * GRID_W + nq, :] = _dot(p_loc.astype(bf16), v) + _dot(p_ctx.astype(bf16), cv)


def _na_latent_attention(qkv, cache_k, cache_v, bias):
    scale = NA_HD ** -0.5
    rb = T_P // SS
    nk = NA_KROWS * GRID_W
    return pl.pallas_call(
        functools.partial(_na_kernel, scale=scale),
        grid=(NA_HEADS, BS),
        in_specs=[pl.BlockSpec((SS, NA_HD), lambda h, b: (rb + b, h)),
                  pl.BlockSpec((SS, NA_HD), lambda h, b: (rb + b, NA_HEADS + h)),
                  pl.BlockSpec((SS, NA_HD), lambda h, b: (rb + b, 2 * NA_HEADS + h)),
                  pl.BlockSpec((1, PAST, NA_HD), lambda h, b: (b, 0, h)),
                  pl.BlockSpec((1, PAST, NA_HD), lambda h, b: (b, 0, h)),
                  pl.BlockSpec((3, 1, NA_QROWS * GRID_W, nk), lambda h, b: (0, h, 0, 0))],
        out_specs=pl.BlockSpec((SS, NA_HD), lambda h, b: (b, h)),
        out_shape=jax.ShapeDtypeStruct((T_S, D), f32),
        compiler_params=_cparams(("parallel", "arbitrary")),
        name="na_attn_latent",
    )(qkv, qkv, qkv, cache_k, cache_v, bias)


def _expand_heads(m):
    return jnp.concatenate([jnp.broadcast_to(m[:, j:j + 1], (m.shape[0], SSD_P)) for j in range(SSD_HPG)], axis=1)


def _ssd_kernel(*refs, seq, has_h0, want_state):
    refs = list(refs)
    (x_ref, b_ref, c_ref, dt_ref, cwx_ref, cwb_ref, cwc_ref, cbx_ref, cbb_ref, cbc_ref,
     alog_ref, dtb_ref, dsk_ref) = refs[:13]
    refs = refs[13:]
    h0_ref = refs.pop(0) if has_h0 else None
    y_ref = refs.pop(0)
    hT_ref = refs.pop(0) if want_state else None
    xp_ref, bp_ref, cp_ref, xs_ref, bs_ref, cs_ref, h_ref = refs
    g = pl.program_id(1)
    nc = seq // SSD_Q
    halo = 8

    for src, pad, dst, cw_ref, cb_ref in ((x_ref, xp_ref, xs_ref, cwx_ref, cbx_ref),
                                          (b_ref, bp_ref, bs_ref, cwb_ref, cbb_ref),
                                          (c_ref, cp_ref, cs_ref, cwc_ref, cbc_ref)):
        width = src.shape[1]
        pad[0:halo, :] = jnp.zeros((halo, width), f32)
        pad[halo + seq:2 * halo + seq, :] = jnp.zeros((halo, width), f32)
        pad[halo:halo + seq, :] = src[...]
        cw = cw_ref[...]
        cb = cb_ref[...]

        def conv_chunk(c, carry, pad=pad, dst=dst, cw=cw, cb=cb, width=width):
            base = pl.multiple_of(c * SSD_Q, SSD_Q)
            win = pad[pl.ds(base, SSD_Q + 2 * halo), :]
            acc = jnp.broadcast_to(cb, (SSD_Q, width))
            for t in range(SSD_CONV_W):
                off = halo - SSD_CONV_W // 2 + t
                sh = pltpu.roll(win, SSD_Q + 2 * halo - off, axis=0)[:SSD_Q]
                acc = acc + sh * cw[t:t + 1, :]
            dst[pl.ds(base, SSD_Q), :] = _silu(acc)
            return carry

        lax.fori_loop(0, nc, conv_chunk, 0)

    row = lax.broadcasted_iota(jnp.int32, (SSD_Q, SSD_Q), 0)
    col = lax.broadcasted_iota(jnp.int32, (SSD_Q, SSD_Q), 1)
    tri = {0: (col <= row), 1: (col >= row)}
    a_neg = -jnp.exp(alog_ref[...])
    dtb = dtb_ref[...]
    dsk = dsk_ref[...]

    for direction in range(2):
        cum_mat = tri[direction].astype(f32)
        first_lane = direction * SSD_HEADS + g * SSD_HPG
        sel = ((col < SSD_HPG) & (row == first_lane + col)).astype(f32)
        if has_h0:
            h_ref[...] = h0_ref[0, direction]
        else:
            h_ref[...] = jnp.zeros_like(h_ref)

        def chunk(ci, carry, direction=direction, cum_mat=cum_mat, sel=sel):
            c = ci if direction == 0 else nc - 1 - ci
            base = pl.multiple_of(c * SSD_Q, SSD_Q)
            rows = pl.ds(base, SSD_Q)
            xc = xs_ref[rows, :]
            bc = bs_ref[rows, :].astype(bf16)
            cc = cs_ref[rows, :].astype(bf16)
            dt_all = dt_ref[rows, :] + dtb
            dt_all = jnp.maximum(dt_all, 0.0) + jnp.log1p(jnp.exp(-jnp.abs(dt_all)))
            dt_h = _dot_exact(dt_all, sel)
            a_h = _dot_exact(dt_all * a_neg, sel)
            ac_h = _dot_exact(cum_mat, a_h)
            ac_t = ac_h.T
            tot_h = ac_h[SSD_Q - 1:SSD_Q, :] if direction == 0 else ac_h[0:1, :]
            xdt = xc * _expand_heads(dt_h)
            cb = _dot_nt(cc, bc)
            y = xc * dsk if direction == 0 else jnp.zeros((SSD_Q, SSD_GW), f32)
            parts = []
            for j in range(SSD_HPG):
                seg = ac_h[:, j:j + 1] - ac_t[j:j + 1, :]
                lmat = jnp.where(tri[direction], jnp.exp(jnp.where(tri[direction], seg, 0.0)), 0.0)
                m = (cb * lmat).astype(bf16)
                parts.append(_dot(m, xdt[:, j * SSD_P:(j + 1) * SSD_P].astype(bf16)))
            y = y + jnp.concatenate(parts, axis=1)
            h = h_ref[...]
            y = y + _dot_nt(cc, h.astype(bf16)) * _expand_heads(jnp.exp(ac_h))
            to_end = jnp.exp(tot_h - ac_h)
            st = _dot_tn((xdt * _expand_heads(to_end)).astype(bf16), bc)
            decay = jnp.exp(tot_h)
            for j in range(SSD_HPG):
                sl = slice(j * SSD_P, (j + 1) * SSD_P)
                h_ref[sl, :] = h[sl, :] * decay[:, j:j + 1] + st[sl, :]
            if direction == 0:
                y_ref[rows, :] = y
            else:
                y_ref[rows, :] += y
            return carry

        lax.fori_loop(0, nc, chunk, 0)
        if want_state:
            hT_ref[0, direction] = h_ref[...]


def _ssd_scan(proj, conv_w, conv_b, a_log, dt_bias, d_skip, h0, *, latent):
    if latent:
        nb, seq, row0 = BS, SS, T_P
    else:
        nb, seq, row0 = BP, SP, 0
    rb = row0 // seq
    xb0 = SSD_DI // SSD_GW
    bb0 = 2 * SSD_DI // SSD_N
    cb0 = bb0 + SSD_G
    dtb0 = (SSD_DI + SSD_CONV_CH) // V7X_LANES
    cwb0 = SSD_DI // SSD_N
    row = lambda blk: (lambda b, g: (rb + b, blk(g)))
    specs = [pl.BlockSpec((seq, SSD_GW), row(lambda g: xb0 + g)),
             pl.BlockSpec((seq, SSD_N), row(lambda g: bb0 + g)),
             pl.BlockSpec((seq, SSD_N), row(lambda g: cb0 + g)),
             pl.BlockSpec((seq, V7X_LANES), row(lambda g: dtb0)),
             pl.BlockSpec((SSD_CONV_W, SSD_GW), lambda b, g: (0, g)),
             pl.BlockSpec((SSD_CONV_W, SSD_N), lambda b, g: (0, cwb0 + g)),
             pl.BlockSpec((SSD_CONV_W, SSD_N), lambda b, g: (0, cwb0 + SSD_G + g)),
             pl.BlockSpec((1, SSD_GW), lambda b, g: (0, g)),
             pl.BlockSpec((1, SSD_N), lambda b, g: (0, cwb0 + g)),
             pl.BlockSpec((1, SSD_N), lambda b, g: (0, cwb0 + SSD_G + g)),
             pl.BlockSpec((1, V7X_LANES), lambda b, g: (0, 0)),
             pl.BlockSpec((1, V7X_LANES), lambda b, g: (0, 0)),
             pl.BlockSpec((1, SSD_GW), lambda b, g: (0, g))]
    cbr = conv_b.reshape(1, SSD_CONV_CH)
    ins = [proj, proj, proj, proj, conv_w, conv_w, conv_w, cbr, cbr, cbr,
           a_log.reshape(1, 2 * SSD_HEADS), dt_bias.reshape(1, 2 * SSD_HEADS),
           jnp.repeat(d_skip, SSD_P).reshape(1, SSD_DI)]
    state_spec = pl.BlockSpec((1, 2, SSD_GW, SSD_N), lambda b, g: (b, 0, g, 0))
    if latent:
        specs.append(state_spec)
        ins.append(h0)
    out_shape = [jax.ShapeDtypeStruct((nb * seq, SSD_DI), f32)]
    out_specs = [pl.BlockSpec((seq, SSD_GW), lambda b, g: (b, g))]
    if not latent:
        out_shape.append(jax.ShapeDtypeStruct((nb, 2, SSD_HEADS * SSD_P, SSD_N), f32))
        out_specs.append(state_spec)
    pad_rows = seq + 16
    outs = pl.pallas_call(
        functools.partial(_ssd_kernel, seq=seq, has_h0=latent, want_state=not latent),
        grid=(nb, SSD_G), in_specs=specs, out_specs=out_specs, out_shape=out_shape,
        scratch_shapes=[pltpu.VMEM((pad_rows, SSD_GW), f32), pltpu.VMEM((pad_rows, SSD_N), f32),
                        pltpu.VMEM((pad_rows, SSD_N), f32),
                        pltpu.VMEM((seq, SSD_GW), f32), pltpu.VMEM((seq, SSD_N), f32), pltpu.VMEM((seq, SSD_N), f32),
                        pltpu.VMEM((SSD_GW, SSD_N), f32)],
        compiler_params=_cparams(("parallel", "arbitrary")),
        name="ssd_latent" if latent else "ssd_context",
    )(*ins)
    return outs


def _axial_tables(half):
    t = np.arange(SS)
    n = half // 2
    inv = ROPE_BASE ** (-np.arange(n, dtype=np.float32) / n)
    cos, sin = [], []
    for pos in (t // GRID_W, t % GRID_W):
        ang = pos.astype(np.float32)[:, None] * inv[None].astype(np.float32)
        cos += [np.cos(ang), np.cos(ang)]
        sin += [-np.sin(ang), np.sin(ang)]
    return np.concatenate(cos, axis=1).astype(np.float32), np.concatenate(sin, axis=1).astype(np.float32)


def _token_tables(cos_lat, sin_lat):
    w = cos_lat.shape[1]
    cos = np.concatenate([np.ones((T_P, w), np.float32), np.tile(cos_lat, (BS, 1))], axis=0)
    sin = np.concatenate([np.zeros((T_P, w), np.float32), np.tile(sin_lat, (BS, 1))], axis=0)
    return jnp.asarray(cos), jnp.asarray(sin)


def _rope_tables():
    c128, s128 = _axial_tables(DIFF_HD // 2)
    c64, s64 = _axial_tables(MLA_ROPE // 2)
    one, zero = np.ones((SS, 64), np.float32), np.zeros((SS, 64), np.float32)
    one2, zero2 = np.ones((SS, 128), np.float32), np.zeros((SS, 128), np.float32)
    diff = _token_tables(c128, s128)
    mla_q = _token_tables(np.concatenate([one2, c64, one], 1), np.concatenate([zero2, s64, zero], 1))
    mla_kr = _token_tables(np.concatenate([c64, one], 1), np.concatenate([s64, zero], 1))
    return diff, mla_q, mla_kr


def kernel(x_prompt, x_sample, cache_diff_k, cache_diff_v, cache_na_k, cache_na_v, cache_mla_ckv, cache_mla_krope,
           state_ssd, c, c_ctx, ada_w, ada_b, norm_w, ffn_w_gu, ffn_w_down, final_norm_w, diff_w_qkv, diff_w_o,
           diff_lambda, diff_subln_w, na_w_qkv, na_w_o, na_rpb, mla_w_dq, mla_q_norm, mla_w_uq, mla_w_dkv,
           mla_kv_norm, mla_w_ukv, mla_w_o, ssd_w_in, ssd_conv_w, ssd_conv_b, ssd_a_log, ssd_dt_bias, ssd_d,
           ssd_norm_w, ssd_w_out):
    x = jnp.concatenate([x_prompt.reshape(T_P, D), x_sample.reshape(T_S, D)], axis=0)
    cond = jnp.concatenate([c_ctx[None], c, jnp.zeros((COND_PAD - N_COND, D), f32)], axis=0)
    mods = _modulation(cond, ada_w, ada_b)[:, :N_COND].reshape(DEPTH, N_COND, 3, 3, 1, D)
    rope_diff, rope_mla_q, rope_mla_kr = _rope_tables()

    def mod(i, k, what):
        return mods[i, :, k, what]

    def o_proj(a, w_o, i, **kw):
        return _fused_matmul(a, w_o.astype(bf16), epi="residual", epi_args=(x, mod(i, 1, 2)), tn=1024, **kw)

    outs = {}
    for i in range(DEPTH):
        x = _ffn(x, norm_w[i, 0], mod(i, 0, 0), mod(i, 0, 1), mod(i, 0, 2), ffn_w_gu[i, 0].astype(bf16),
                 ffn_w_down[i, 0].astype(bf16), final_norm_w, final_norm=False)
        pre_args = (norm_w[i, 1], mod(i, 1, 0), mod(i, 1, 1))
        kind = i % 4
        if kind == 0:
            lam_init = 0.8 - 0.6 * math.exp(-0.3 * i)
            qkv = _fused_matmul(x, diff_w_qkv[0].astype(bf16), pre="mod", pre_args=pre_args, epi="rope",
                                epi_args=rope_diff, rope=(2 * D // 512, 0, 512, DIFF_HD // 4), name="diff_qkv")
            outs["diff_k"] = qkv[:T_P, D:2 * D].reshape(BP, 1, SP, DIFF_HEADS, 2, DIFF_HD)
            outs["diff_v"] = qkv[:T_P, 2 * D:].reshape(BP, 1, SP, DIFF_HEADS, 2 * DIFF_HD)
            o_p = _diff_attention(qkv, diff_lambda[0], diff_subln_w[0], None, None, lam_init=lam_init, latent=False)
            o_s = _diff_attention(qkv, diff_lambda[0], diff_subln_w[0], cache_diff_k.reshape(BS, PAST, D),
                                  cache_diff_v.reshape(BS, PAST, D), lam_init=lam_init, latent=True)
            x = o_proj(jnp.concatenate([o_p, o_s], axis=0), diff_w_o[0], i, name="diff_out")
        elif kind == 1:
            qkv = _fused_matmul(x, na_w_qkv[0].astype(bf16), pre="mod", pre_args=pre_args, name="na_qkv")
            outs["na_k"] = qkv[:T_P, D:2 * D].reshape(BP, 1, SP, NA_HEADS, NA_HD)
            outs["na_v"] = qkv[:T_P, 2 * D:].reshape(BP, 1, SP, NA_HEADS, NA_HD)
            o_p = _context_attention(qkv, heads=NA_HEADS, hd=NA_HD, scale=NA_HD ** -0.5)
            o_s = _na_latent_attention(qkv, cache_na_k.reshape(BS, PAST, D), cache_na_v.reshape(BS, PAST, D),
                                       _na_bias_tables(na_rpb[0]))
            x = o_proj(jnp.concatenate([o_p, o_s], axis=0), na_w_o[0], i, name="na_out")
        elif kind == 2:
            w_lat = jnp.concatenate([mla_w_dq[0], mla_w_dkv[0],
                                     jnp.zeros((D, V7X_LANES - MLA_ROPE), f32)], axis=1).astype(bf16)
            n_lat = w_lat.shape[1]
            lat = _fused_matmul(x, w_lat, pre="mod", pre_args=pre_args, epi="rope", epi_args=rope_mla_kr,
                                rope=(1, n_lat - V7X_LANES, n_lat, MLA_ROPE // 4), tn=n_lat, name="mla_down")
            w_uq = jnp.pad(mla_w_uq[0].reshape(MLA_Q_LORA, MLA_HEADS, MLA_NOPE + MLA_ROPE),
                           ((0, 0), (0, 0), (0, MLA_QW - MLA_NOPE - MLA_ROPE))).reshape(MLA_Q_LORA, MLA_HEADS * MLA_QW)
            q = _fused_matmul(lat, w_uq.astype(bf16), pre="rms", pre_args=(mla_q_norm[0],), epi="rope",
                              epi_args=rope_mla_q, rope=(MLA_HEADS * MLA_QW // 512, 0, 512, MLA_ROPE // 4),
                              name="mla_q")
            w_ukv = mla_w_ukv[0].astype(bf16)
            kvx, ckv_n = _fused_matmul(lat, w_ukv, k_blk=MLA_Q_LORA // MLA_KV_LORA, pre="rms",
                                       pre_args=(mla_kv_norm[0],), want_pre_out=True, name="mla_kv")
            kvc = _fused_matmul(cache_mla_ckv.reshape(BS * PAST, MLA_KV_LORA), w_ukv, name="mla_kv_cache")
            krc = jnp.pad(cache_mla_krope.reshape(BS * PAST, MLA_ROPE), ((0, 0), (0, V7X_LANES - MLA_ROPE)))
            outs["mla_ckv"] = ckv_n[:T_P].reshape(BP, 1, SP, MLA_KV_LORA)
            outs["mla_kr"] = lat[:T_P, MLA_Q_LORA + MLA_KV_LORA:MLA_Q_LORA + MLA_KV_LORA + MLA_ROPE].reshape(
                BP, 1, SP, MLA_ROPE)
            o_p = _mla_attention(q, kvx, lat, None, None, latent=False)
            o_s = _mla_attention(q, kvx, lat, kvc, krc, latent=True)
            x = o_proj(jnp.concatenate([o_p, o_s], axis=0), mla_w_o[0], i, name="mla_out")
        else:
            proj = _fused_matmul(x, ssd_w_in[0].astype(bf16), pre="mod", pre_args=pre_args, tn=1152, name="ssd_in")
            args = (proj, ssd_conv_w[0], ssd_conv_b[0], ssd_a_log[0], ssd_dt_bias[0], ssd_d[0])
            y_p, h_t = _ssd_scan(*args, None, latent=False)
            (y_s,) = _ssd_scan(*args, state_ssd.reshape(BS, 2, SSD_HEADS * SSD_P, SSD_N), latent=True)
            outs["ssd_h"] = h_t.reshape(BP, 1, 2, SSD_HEADS, SSD_P, SSD_N)
            x = o_proj(jnp.concatenate([y_p, y_s], axis=0), ssd_w_out[0], i, pre="gated_rms",
                       pre_args=(proj, ssd_norm_w[0]), tm=TM // 2, name="ssd_out")
        x = _ffn(x, norm_w[i, 2], mod(i, 2, 0), mod(i, 2, 1), mod(i, 2, 2), ffn_w_gu[i, 1].astype(bf16),
                 ffn_w_down[i, 1].astype(bf16), final_norm_w, final_norm=(i == DEPTH - 1))
    return (x[:T_P].reshape(BP, SP, D), x[T_P:].reshape(BS, SS, D), outs["diff_k"], outs["diff_v"], outs["na_k"],
            outs["na_v"], outs["mla_ckv"], outs["mla_kr"], outs["ssd_h"])
```

```python
import functools
import math

import numpy as np
import jax
import jax.numpy as jnp
from jax import lax
from jax.experimental import pallas as pl
from jax.experimental.pallas import tpu as pltpu

f32 = jnp.float32
bf16 = jnp.bfloat16

D = 2048
BP, SP = 16, 256
BS, SS = 8, 2048
PAST = 512
GRID_W = 64
GRID_R = SS // GRID_W
DEPTH = 4
D_FF = 5632
EPS = 1e-6
ROPE_BASE = 10000.0
T_P = BP * SP
T_S = BS * SS
T = T_P + T_S
N_COND = 1 + BS
COND_PAD = 16

DIFF_HEADS, DIFF_HD = 8, 128
NA_HEADS, NA_HD = 16, 128
NA_WIN_ROWS, NA_WIN_COLS = 8, 16
NA_QROWS = 4
NA_KROWS = 12
MLA_HEADS, MLA_Q_LORA, MLA_KV_LORA = 16, 512, 256
MLA_NOPE, MLA_ROPE, MLA_V = 128, 64, 128
MLA_QW = 256
SSD_DI = 2 * D
SSD_P, SSD_HEADS, SSD_G, SSD_HPG, SSD_N = 64, 64, 8, 8, 128
SSD_CONV_W, SSD_Q = 7, 128
SSD_CONV_CH = SSD_DI + 2 * SSD_G * SSD_N
SSD_IN = SSD_DI + SSD_CONV_CH + 2 * SSD_HEADS
SSD_GW = SSD_HPG * SSD_P

V7X_LANES = 128
V7X_VMEM_LIMIT = 60000 * 1024

TM = 512
QKV_TM, QKV_TN = 1024, 1024


def _cparams(sem):
    return pltpu.CompilerParams(dimension_semantics=sem, vmem_limit_bytes=V7X_VMEM_LIMIT)


def _mod_row(i, tm):
    start = i * tm
    return jnp.where(start < T_P, 0, 1 + (start - T_P) // SS)


def _silu(x):
    return x / (1.0 + jnp.exp(-x))


def _for_row_chunks(n_rows, fn, chunk=256):
    def body(c, carry):
        fn(pl.ds(pl.multiple_of(c * chunk, chunk), chunk))
        return carry

    lax.fori_loop(0, n_rows // chunk, body, 0)


def _call_into(kernel_fn, prev, ins, *, in_specs, **kw):
    if prev is None:
        return pl.pallas_call(kernel_fn, in_specs=in_specs, **kw)(*ins)
    n = len(ins)

    def body(*refs):
        kernel_fn(*refs[:n], *refs[n + 1:])

    return pl.pallas_call(body, in_specs=list(in_specs) + [pl.BlockSpec(memory_space=pl.ANY)],
                          input_output_aliases={n: 0}, **kw)(*ins, prev)


def _dot(a, b):
    return jnp.dot(a, b, preferred_element_type=f32)


def _dot_nt(a, b):
    return lax.dot_general(a, b, (((1,), (1,)), ((), ())), preferred_element_type=f32)


def _dot_tn(a, b):
    return lax.dot_general(a, b, (((0,), (0,)), ((), ())), preferred_element_type=f32)


def _dot_exact(a, b):
    return jnp.dot(a, b, preferred_element_type=f32, precision=lax.Precision.HIGHEST)


def _mod_kernel(c_ref, w_ref, b_ref, o_ref):
    s = _silu(c_ref[...])
    o_ref[0] = jnp.dot(s, w_ref[0], preferred_element_type=f32, precision=lax.Precision.HIGHEST) + b_ref[0]


def _modulation(cond, ada_w, ada_b):
    n = ada_w.shape[-1]
    tn = 1024
    return pl.pallas_call(
        _mod_kernel,
        grid=(DEPTH, n // tn),
        in_specs=[pl.BlockSpec((COND_PAD, D), lambda l, j: (0, 0)),
                  pl.BlockSpec((1, D, tn), lambda l, j: (l, 0, j)),
                  pl.BlockSpec((1, 1, tn), lambda l, j: (l, 0, j))],
        out_specs=pl.BlockSpec((1, COND_PAD, tn), lambda l, j: (l, 0, j)),
        out_shape=jax.ShapeDtypeStruct((DEPTH, COND_PAD, n), f32),
        compiler_params=_cparams(("arbitrary", "arbitrary")),
        name="modulation",
    )(cond, ada_w, ada_b.reshape(DEPTH, 1, n))


def _rope_apply(acc, cos, sin, shift):
    w = acc.shape[1]
    reps = w // cos.shape[1]
    if reps > 1:
        cos = jnp.concatenate([cos] * reps, axis=1)
        sin = jnp.concatenate([sin] * reps, axis=1)
    lane = lax.broadcasted_iota(jnp.int32, acc.shape, 1)
    first = (lane % (2 * shift)) < shift
    rot = jnp.where(first, pltpu.roll(acc, w - shift, axis=1), pltpu.roll(acc, shift, axis=1))
    return acc * cos + rot * sin


def _fused_matmul_kernel(*refs, pre, epi, rope, want_pre_out, res_scale):
    refs = list(refs)
    a_ref = refs.pop(0)
    if pre == "mod":
        nw_ref, sh_ref, sc_ref = refs.pop(0), refs.pop(0), refs.pop(0)
    elif pre == "rms":
        nw_ref = refs.pop(0)
    elif pre == "gated_rms":
        z_ref, nw_ref = refs.pop(0), refs.pop(0)
    w_ref = refs.pop(0)
    if epi == "rope":
        cos_ref, sin_ref = refs.pop(0), refs.pop(0)
    elif epi == "residual":
        res_ref, gate_ref = refs.pop(0), refs.pop(0)
    o_ref = refs.pop(0)
    if want_pre_out:
        po_ref = refs.pop(0)
    h_ref = refs.pop(0)
    j = pl.program_id(1)

    @pl.when(j == 0)
    def _():
        def prologue(rows):
            a = a_ref[rows, :].astype(f32)
            if pre == "gated_rms":
                a = a * _silu(z_ref[rows, :])
            if pre != "none":
                a = a * lax.rsqrt(jnp.mean(a * a, axis=-1, keepdims=True) + EPS) * nw_ref[...]
            if pre == "mod":
                a = a * (1.0 + sc_ref[0]) + sh_ref[0]
            if want_pre_out:
                po_ref[rows, :] = a
            h_ref[rows, :] = a.astype(bf16)

        _for_row_chunks(a_ref.shape[0], prologue)

    acc = _dot(h_ref[...], w_ref[...])
    if epi == "rope":
        n_tiles, lo, hi, shift = rope

        @pl.when(j < n_tiles)
        def _():
            if lo == 0 and hi == acc.shape[1]:
                o_ref[...] = _rope_apply(acc, cos_ref[...], sin_ref[...], shift)
            else:
                mid = _rope_apply(acc[:, lo:hi], cos_ref[...], sin_ref[...], shift)
                o_ref[...] = jnp.concatenate([acc[:, :lo], mid] + ([acc[:, hi:]] if hi < acc.shape[1] else []), axis=1)

        @pl.when(j >= n_tiles)
        def _():
            o_ref[...] = acc
    elif epi == "residual":
        o_ref[...] = res_ref[...] + (res_scale * gate_ref[0]) * acc
    else:
        o_ref[...] = acc.astype(o_ref.dtype)


def _fused_matmul(a, w, *, k_blk=0, pre="none", pre_args=(), epi="none", epi_args=(), rope=None,
                  want_pre_out=False, res_scale=1.0, tm=TM, tn=512, name="fused_matmul"):
    rows = a.shape[0]
    k, n = w.shape
    assert rows % tm == 0 and n % tn == 0
    grid = (rows // tm, n // tn)
    row_k = lambda i, j: (i, k_blk)
    vec_k = pl.BlockSpec((1, k), lambda i, j: (0, 0))
    mod_k = pl.BlockSpec((1, 1, k), lambda i, j: (_mod_row(i, tm), 0, 0))
    ins, specs = [a], [pl.BlockSpec((tm, k), row_k)]
    if pre == "mod":
        nw, sh, sc = pre_args
        ins += [nw.reshape(1, k), sh, sc]
        specs += [vec_k, mod_k, mod_k]
    elif pre == "rms":
        ins += [pre_args[0].reshape(1, k)]
        specs += [vec_k]
    elif pre == "gated_rms":
        z, nw = pre_args
        ins += [z, nw.reshape(1, k)]
        specs += [pl.BlockSpec((tm, k), lambda i, j: (i, 0)), vec_k]
    ins.append(w)
    if n == tn:
        specs.append(pl.BlockSpec((k, tn), lambda i, j: (0, 0), pipeline_mode=pl.Buffered(1)))
    else:
        specs.append(pl.BlockSpec((k, tn), lambda i, j: (0, j)))
    if epi == "rope":
        cos, sin = epi_args
        p = cos.shape[1]
        ins += [cos, sin]
        specs += [pl.BlockSpec((tm, p), lambda i, j: (i, 0))] * 2
    elif epi == "residual":
        res, gate = epi_args
        ins += [res, gate]
        specs += [pl.BlockSpec((tm, tn), lambda i, j: (i, j)),
                  pl.BlockSpec((1, 1, tn), lambda i, j: (_mod_row(i, tm), 0, j))]
    out_shape = [jax.ShapeDtypeStruct((rows, n), f32)]
    out_specs = [pl.BlockSpec((tm, tn), lambda i, j: (i, j))]
    if want_pre_out:
        out_shape.append(jax.ShapeDtypeStruct((rows, k), f32))
        out_specs.append(pl.BlockSpec((tm, k), lambda i, j: (i, 0)))
    outs = pl.pallas_call(
        functools.partial(_fused_matmul_kernel, pre=pre, epi=epi, rope=rope, want_pre_out=want_pre_out,
                          res_scale=res_scale),
        grid=grid, in_specs=specs, out_specs=out_specs, out_shape=out_shape,
        scratch_shapes=[pltpu.VMEM((tm, k), bf16)],
        compiler_params=_cparams(("parallel", "arbitrary")),
        name=name,
    )(*ins)
    return outs if want_pre_out else outs[0]


def _ffn_kernel(x_ref, nw_ref, sh_ref, sc_ref, gate_ref, wg_ref, wu_ref, wd_ref, fw_ref, o_ref, h_ref, acc_ref, *,
                final_norm):
    f = pl.program_id(1)

    @pl.when(f == 0)
    def _():
        def prologue(rows):
            x = x_ref[rows, :]
            h = x * lax.rsqrt(jnp.mean(x * x, axis=-1, keepdims=True) + EPS) * nw_ref[...]
            h_ref[rows, :] = (h * (1.0 + sc_ref[0]) + sh_ref[0]).astype(bf16)

        _for_row_chunks(x_ref.shape[0], prologue)
        acc_ref[...] = jnp.zeros_like(acc_ref)

    h = h_ref[...]
    g = _dot(h, wg_ref[...])
    u = _dot(h, wu_ref[...])
    acc_ref[...] += _dot((_silu(g) * u).astype(bf16), wd_ref[...])

    @pl.when(f == pl.num_programs(1) - 1)
    def _():
        def epilogue(rows):
            y = x_ref[rows, :] + (0.5 * gate_ref[0]) * acc_ref[rows, :]
            if final_norm:
                y = y * lax.rsqrt(jnp.mean(y * y, axis=-1, keepdims=True) + EPS) * fw_ref[...]
            o_ref[rows, :] = y

        _for_row_chunks(x_ref.shape[0], epilogue)


def _ffn(x, nw, sh, sc, gate, wgu, wd, final_w, *, final_norm, tm=TM, tf=512):
    nf = D_FF // tf
    vec = pl.BlockSpec((1, D), lambda i, f: (0, 0))
    mod = pl.BlockSpec((1, 1, D), lambda i, f: (_mod_row(i, tm), 0, 0))
    return pl.pallas_call(
        functools.partial(_ffn_kernel, final_norm=final_norm),
        grid=(T // tm, nf),
        in_specs=[pl.BlockSpec((tm, D), lambda i, f: (i, 0)), vec, mod, mod, mod,
                  pl.BlockSpec((D, tf), lambda i, f: (0, f)),
                  pl.BlockSpec((D, tf), lambda i, f: (0, f + nf)),
                  pl.BlockSpec((tf, D), lambda i, f: (f, 0)),
                  vec],
        out_specs=pl.BlockSpec((tm, D), lambda i, f: (i, 0)),
        out_shape=jax.ShapeDtypeStruct((T, D), f32),
        scratch_shapes=[pltpu.VMEM((tm, D), bf16), pltpu.VMEM((tm, D), f32)],
        compiler_params=_cparams(("parallel", "arbitrary")),
        name="ffn",
    )(x, nw.reshape(1, D), sh, sc, gate, wgu, wgu, wd, final_w.reshape(1, D))


def _softmax_segments(segs):
    m = functools.reduce(jnp.maximum, [jnp.max(s, axis=-1, keepdims=True) for s in segs])
    es = [jnp.exp(s - m) for s in segs]
    l = functools.reduce(jnp.add, [jnp.sum(e, axis=-1, keepdims=True) for e in es])
    inv = 1.0 / l
    return [e * inv for e in es]


def _diff_attn_kernel(*refs, scale, lam_init, has_cache):
    if has_cache:
        lam_ref, sub_ref, q_ref, k_ref, v_ref, kc_ref, vc_ref, o_ref = refs
    else:
        lam_ref, sub_ref, q_ref, k_ref, v_ref, o_ref = refs
    lp = lam_ref[...]
    lam = (jnp.exp(jnp.sum(lp[0:1] * lp[1:2], axis=-1, keepdims=True))
           - jnp.exp(jnp.sum(lp[2:3] * lp[3:4], axis=-1, keepdims=True)) + lam_init)
    q = q_ref[...]
    ks = [k_ref[...]] + ([kc_ref[0]] if has_cache else [])
    vs = [v_ref[...]] + ([vc_ref[0]] if has_cache else [])
    probs = []
    for m in range(2):
        sl = slice(m * DIFF_HD, (m + 1) * DIFF_HD)
        qm = q[:, sl].astype(bf16)
        probs.append(_softmax_segments([_dot_nt(qm, kk[:, sl].astype(bf16)) * scale for kk in ks]))
    o = None
    for p1, p2, v in zip(probs[0], probs[1], vs):
        part = _dot((p1 - lam * p2).astype(bf16), v.astype(bf16))
        o = part if o is None else o + part
    o = o * lax.rsqrt(jnp.mean(o * o, axis=-1, keepdims=True) + EPS) * sub_ref[...]
    o_ref[...] = o * (1.0 - lam_init)


def _diff_attention(qkv, lam_p, subln, cache_k, cache_v, *, lam_init, latent, prev=None, tq=256):
    hw = 2 * DIFF_HD
    scale = DIFF_HD ** -0.5
    if latent:
        nb, seq, row0 = BS, SS, T_P
    else:
        nb, seq, row0 = BP, SP, 0
    nq = seq // tq
    small = [pl.BlockSpec((4, DIFF_HD), lambda b, h, i: (0, 0)), pl.BlockSpec((1, hw), lambda b, h, i: (0, 0))]
    specs = small + [
        pl.BlockSpec((tq, hw), lambda b, h, i: ((row0 + b * seq) // tq + i, h)),
        pl.BlockSpec((seq, hw), lambda b, h, i: (row0 // seq + b, DIFF_HEADS + h)),
        pl.BlockSpec((seq, hw), lambda b, h, i: (row0 // seq + b, 2 * DIFF_HEADS + h))]
    ins = [lam_p, subln.reshape(1, hw), qkv, qkv, qkv]
    if latent:
        specs += [pl.BlockSpec((1, PAST, hw), lambda b, h, i: (b, 0, h))] * 2
        ins += [cache_k, cache_v]
    return _call_into(
        functools.partial(_diff_attn_kernel, scale=scale, lam_init=lam_init, has_cache=latent), prev, ins,
        grid=(nb, DIFF_HEADS, nq), in_specs=specs,
        out_specs=pl.BlockSpec((tq, hw), lambda b, h, i: ((row0 + b * seq) // tq + i, h)),
        out_shape=jax.ShapeDtypeStruct((T, D), f32),
        compiler_params=_cparams(("parallel", "parallel", "arbitrary")),
        name="diff_attn_latent" if latent else "diff_attn_context")


def _attn_kernel(q_ref, k_ref, v_ref, o_ref, *, scale):
    s = _dot_nt(q_ref[...].astype(bf16), k_ref[...].astype(bf16)) * scale
    (p,) = _softmax_segments([s])
    o_ref[...] = _dot(p.astype(bf16), v_ref[...].astype(bf16))


def _context_attention(qkv, prev, *, heads, hd, scale):
    return _call_into(
        functools.partial(_attn_kernel, scale=scale), prev, [qkv, qkv, qkv],
        grid=(BP, heads),
        in_specs=[pl.BlockSpec((SP, hd), lambda b, h: (b, h)),
                  pl.BlockSpec((SP, hd), lambda b, h: (b, heads + h)),
                  pl.BlockSpec((SP, hd), lambda b, h: (b, 2 * heads + h))],
        out_specs=pl.BlockSpec((SP, hd), lambda b, h: (b, h)),
        out_shape=jax.ShapeDtypeStruct((T, heads * hd), f32),
        compiler_params=_cparams(("parallel", "parallel")),
        name="context_attn")


def _mla_attn_kernel(*refs, scale, has_cache):
    if has_cache:
        q_ref, kn_ref, kr_ref, v_ref, knc_ref, krc_ref, vc_ref, o_ref = refs
    else:
        q_ref, kn_ref, kr_ref, v_ref, o_ref = refs
    q = q_ref[...].astype(bf16)
    ks = [jnp.concatenate([kn_ref[...], kr_ref[...]], axis=1)]
    vs = [v_ref[...]]
    if has_cache:
        ks.append(jnp.concatenate([knc_ref[...], krc_ref[...]], axis=1))
        vs.append(vc_ref[...])
    ps = _softmax_segments([_dot_nt(q, k.astype(bf16)) * scale for k in ks])
    o = None
    for p, v in zip(ps, vs):
        part = _dot(p.astype(bf16), v.astype(bf16))
        o = part if o is None else o + part
    o_ref[...] = o


def _mla_attention(q, kvx, lat, kvc, krc, *, latent, prev=None, tq=256):
    scale = (MLA_NOPE + MLA_ROPE) ** -0.5
    kr_blk = (MLA_Q_LORA + MLA_KV_LORA) // V7X_LANES
    if latent:
        nb, seq, row0 = BS, SS, T_P
    else:
        nb, seq, row0 = BP, SP, 0
    nq = seq // tq
    specs = [pl.BlockSpec((tq, MLA_QW), lambda b, h, i: ((row0 + b * seq) // tq + i, h)),
             pl.BlockSpec((seq, MLA_NOPE), lambda b, h, i: (row0 // seq + b, 2 * h)),
             pl.BlockSpec((seq, V7X_LANES), lambda b, h, i: (row0 // seq + b, kr_blk)),
             pl.BlockSpec((seq, MLA_V), lambda b, h, i: (row0 // seq + b, 2 * h + 1))]
    ins = [q, kvx, lat, kvx]
    if latent:
        specs += [pl.BlockSpec((PAST, MLA_NOPE), lambda b, h, i: (b, 2 * h)),
                  pl.BlockSpec((PAST, V7X_LANES), lambda b, h, i: (b, 0)),
                  pl.BlockSpec((PAST, MLA_V), lambda b, h, i: (b, 2 * h + 1))]
        ins += [kvc, krc, kvc]
    return _call_into(
        functools.partial(_mla_attn_kernel, scale=scale, has_cache=latent), prev, ins,
        grid=(nb, MLA_HEADS, nq), in_specs=specs,
        out_specs=pl.BlockSpec((tq, MLA_V), lambda b, h, i: ((row0 + b * seq) // tq + i, h)),
        out_shape=jax.ShapeDtypeStruct((T, MLA_HEADS * MLA_V), f32),
        compiler_params=_cparams(("parallel", "parallel", "arbitrary")),
        name="mla_attn_latent" if latent else "mla_attn_context")


def _na_block_plan():
    plan = []
    for r0 in range(0, GRID_R, NA_QROWS):
        kb = min(max(r0 - NA_WIN_ROWS // 2, 0), GRID_R - NA_KROWS)
        var = 0 if r0 == 0 else (2 if r0 == GRID_R - NA_QROWS else 1)
        plan.append((r0, kb, var))
    return plan


def _na_bias_tables(rpb):
    reps = {var: (r0, kb) for r0, kb, var in _na_block_plan()}
    nr, ncol = 2 * NA_WIN_ROWS - 1, 2 * NA_WIN_COLS - 1
    rsel = np.zeros((3, NA_QROWS, NA_KROWS, nr), np.float32)
    for var in range(3):
        r0, kb = reps[var]
        for i in range(NA_QROWS):
            r = r0 + i
            rs = min(max(r - NA_WIN_ROWS // 2, 0), GRID_R - NA_WIN_ROWS)
            for kk in range(NA_KROWS):
                kr = kb + kk
                if rs <= kr < rs + NA_WIN_ROWS:
                    rsel[var, i, kk, kr - r + NA_WIN_ROWS - 1] = 1.0
    csel = np.zeros((GRID_W, GRID_W, ncol), np.float32)
    for c in range(GRID_W):
        cstart = min(max(c - NA_WIN_COLS // 2, 0), GRID_W - NA_WIN_COLS)
        for kc in range(cstart, cstart + NA_WIN_COLS):
            csel[c, kc, kc - c + NA_WIN_COLS - 1] = 1.0
    ok = (rsel.sum(-1) > 0)[:, None, :, None, :, None] & (csel.sum(-1) > 0)[None, None, None, :, None, :]
    val = jnp.einsum("vikr,hrd->vhikd", rsel, rpb, precision=lax.Precision.HIGHEST)
    tab = jnp.einsum("vhikd,cjd->vhickj", val, csel, precision=lax.Precision.HIGHEST)
    tab = jnp.where(ok, tab, -1e30)
    return tab.reshape(3, NA_HEADS, NA_QROWS * GRID_W, NA_KROWS * GRID_W).astype(f32)


def _na_kernel(q_ref, k_ref, v_ref, ck_ref, cv_ref, bias_ref, o_ref, *, scale):
    ck = ck_ref[0].astype(bf16)
    cv = cv_ref[0].astype(bf16)
    nq = NA_QROWS * GRID_W
    nk = NA_KROWS * GRID_W
    for r0, kb, var in _na_block_plan():
        q = q_ref[r0 * GRID_W:r0 * GRID_W + nq, :].astype(bf16)
        k = k_ref[kb * GRID_W:kb * GRID_W + nk, :].astype(bf16)
        v = v_ref[kb * GRID_W:kb * GRID_W + nk, :].astype(bf16)
        s_loc = _dot_nt(q, k) * scale + bias_ref[var, 0]
        s_ctx = _dot_nt(q, ck) * scale
        p_loc, p_ctx = _softmax_segments([s_loc, s_ctx])
        o_ref[r0 * GRID_W:r0 * GRID_W + nq, :] = _dot(p_loc.astype(bf16), v) + _dot(p_ctx.astype(bf16), cv)


def _na_latent_attention(qkv, cache_k, cache_v, bias):
    scale = NA_HD ** -0.5
    rb = T_P // SS
    nk = NA_KROWS * GRID_W
    return pl.pallas_call(
        functools.partial(_na_kernel, scale=scale),
        grid=(NA_HEADS, BS),
        in_specs=[pl.BlockSpec((SS, NA_HD), lambda h, b: (rb + b, h)),
                  pl.BlockSpec((SS, NA_HD), lambda h, b: (rb + b, NA_HEADS + h)),
                  pl.BlockSpec((SS, NA_HD), lambda h, b: (rb + b, 2 * NA_HEADS + h)),
                  pl.BlockSpec((1, PAST, NA_HD), lambda h, b: (b, 0, h)),
                  pl.BlockSpec((1, PAST, NA_HD), lambda h, b: (b, 0, h)),
                  pl.BlockSpec((3, 1, NA_QROWS * GRID_W, nk), lambda h, b: (0, h, 0, 0))],
        out_specs=pl.BlockSpec((SS, NA_HD), lambda h, b: (rb + b, h)),
        out_shape=jax.ShapeDtypeStruct((T, D), f32),
        compiler_params=_cparams(("parallel", "arbitrary")),
        name="na_attn_latent",
    )(qkv, qkv, qkv, cache_k, cache_v, bias)


def _expand_heads(m):
    return jnp.concatenate([jnp.broadcast_to(m[:, j:j + 1], (m.shape[0], SSD_P)) for j in range(SSD_HPG)], axis=1)


def _ssd_kernel(*refs, seq, has_h0, want_state):
    refs = list(refs)
    (x_ref, b_ref, c_ref, dt_ref, cwx_ref, cwb_ref, cwc_ref, cbx_ref, cbb_ref, cbc_ref,
     alog_ref, dtb_ref, dsk_ref) = refs[:13]
    refs = refs[13:]
    h0_ref = refs.pop(0) if has_h0 else None
    y_ref = refs.pop(0)
    hT_ref = refs.pop(0) if want_state else None
    xp_ref, bp_ref, cp_ref, xs_ref, bs_ref, cs_ref, h_ref = refs
    g = pl.program_id(1)
    nc = seq // SSD_Q
    halo = 8

    for src, pad, dst, cw_ref, cb_ref in ((x_ref, xp_ref, xs_ref, cwx_ref, cbx_ref),
                                          (b_ref, bp_ref, bs_ref, cwb_ref, cbb_ref),
                                          (c_ref, cp_ref, cs_ref, cwc_ref, cbc_ref)):
        width = src.shape[1]
        pad[0:halo, :] = jnp.zeros((halo, width), f32)
        pad[halo + seq:2 * halo + seq, :] = jnp.zeros((halo, width), f32)
        pad[halo:halo + seq, :] = src[...]
        cw = cw_ref[...]
        cb = cb_ref[...]

        def conv_chunk(c, carry, pad=pad, dst=dst, cw=cw, cb=cb, width=width):
            base = pl.multiple_of(c * SSD_Q, SSD_Q)
            win = pad[pl.ds(base, SSD_Q + 2 * halo), :]
            acc = jnp.broadcast_to(cb, (SSD_Q, width))
            for t in range(SSD_CONV_W):
                off = halo - SSD_CONV_W // 2 + t
                sh = pltpu.roll(win, SSD_Q + 2 * halo - off, axis=0)[:SSD_Q]
                acc = acc + sh * cw[t:t + 1, :]
            dst[pl.ds(base, SSD_Q), :] = _silu(acc)
            return carry

        lax.fori_loop(0, nc, conv_chunk, 0)

    row = lax.broadcasted_iota(jnp.int32, (SSD_Q, SSD_Q), 0)
    col = lax.broadcasted_iota(jnp.int32, (SSD_Q, SSD_Q), 1)
    tri = {0: (col <= row), 1: (col >= row)}
    a_neg = -jnp.exp(alog_ref[...])
    dtb = dtb_ref[...]
    dsk = dsk_ref[...]

    for direction in range(2):
        cum_mat = tri[direction].astype(f32)
        first_lane = direction * SSD_HEADS + g * SSD_HPG
        sel = ((col < SSD_HPG) & (row == first_lane + col)).astype(f32)
        if has_h0:
            h_ref[...] = h0_ref[0, direction]
        else:
            h_ref[...] = jnp.zeros_like(h_ref)

        def chunk(ci, carry, direction=direction, cum_mat=cum_mat, sel=sel):
            c = ci if direction == 0 else nc - 1 - ci
            base = pl.multiple_of(c * SSD_Q, SSD_Q)
            rows = pl.ds(base, SSD_Q)
            xc = xs_ref[rows, :]
            bc = bs_ref[rows, :].astype(bf16)
            cc = cs_ref[rows, :].astype(bf16)
            dt_all = dt_ref[rows, :] + dtb
            dt_all = jnp.maximum(dt_all, 0.0) + jnp.log1p(jnp.exp(-jnp.abs(dt_all)))
            dt_h = _dot_exact(dt_all, sel)
            a_h = _dot_exact(dt_all * a_neg, sel)
            ac_h = _dot_exact(cum_mat, a_h)
            ac_t = ac_h.T
            tot_h = ac_h[SSD_Q - 1:SSD_Q, :] if direction == 0 else ac_h[0:1, :]
            xdt = xc * _expand_heads(dt_h)
            cb = _dot_nt(cc, bc)
            y = xc * dsk if direction == 0 else jnp.zeros((SSD_Q, SSD_GW), f32)
            parts = []
            for j in range(SSD_HPG):
                seg = ac_h[:, j:j + 1] - ac_t[j:j + 1, :]
                lmat = jnp.where(tri[direction], jnp.exp(jnp.where(tri[direction], seg, 0.0)), 0.0)
                m = (cb * lmat).astype(bf16)
                parts.append(_dot(m, xdt[:, j * SSD_P:(j + 1) * SSD_P].astype(bf16)))
            y = y + jnp.concatenate(parts, axis=1)
            h = h_ref[...]
            y = y + _dot_nt(cc, h.astype(bf16)) * _expand_heads(jnp.exp(ac_h))
            to_end = jnp.exp(tot_h - ac_h)
            st = _dot_tn((xdt * _expand_heads(to_end)).astype(bf16), bc)
            decay = jnp.exp(tot_h)
            for j in range(SSD_HPG):
                sl = slice(j * SSD_P, (j + 1) * SSD_P)
                h_ref[sl, :] = h[sl, :] * decay[:, j:j + 1] + st[sl, :]
            if direction == 0:
                y_ref[rows, :] = y
            else:
                y_ref[rows, :] += y
            return carry

        lax.fori_loop(0, nc, chunk, 0)
        if want_state:
            hT_ref[0, direction] = h_ref[...]


def _ssd_scan(proj, conv_w, conv_b, a_log, dt_bias, d_skip, h0, *, latent, prev=None):
    if latent:
        nb, seq, row0 = BS, SS, T_P
    else:
        nb, seq, row0 = BP, SP, 0
    rb = row0 // seq
    xb0 = SSD_DI // SSD_GW
    bb0 = 2 * SSD_DI // SSD_N
    cb0 = bb0 + SSD_G
    dtb0 = (SSD_DI + SSD_CONV_CH) // V7X_LANES
    cwb0 = SSD_DI // SSD_N
    row = lambda blk: (lambda b, g: (rb + b, blk(g)))
    specs = [pl.BlockSpec((seq, SSD_GW), row(lambda g: xb0 + g)),
             pl.BlockSpec((seq, SSD_N), row(lambda g: bb0 + g)),
             pl.BlockSpec((seq, SSD_N), row(lambda g: cb0 + g)),
             pl.BlockSpec((seq, V7X_LANES), row(lambda g: dtb0)),
             pl.BlockSpec((SSD_CONV_W, SSD_GW), lambda b, g: (0, g)),
             pl.BlockSpec((SSD_CONV_W, SSD_N), lambda b, g: (0, cwb0 + g)),
             pl.BlockSpec((SSD_CONV_W, SSD_N), lambda b, g: (0, cwb0 + SSD_G + g)),
             pl.BlockSpec((1, SSD_GW), lambda b, g: (0, g)),
             pl.BlockSpec((1, SSD_N), lambda b, g: (0, cwb0 + g)),
             pl.BlockSpec((1, SSD_N), lambda b, g: (0, cwb0 + SSD_G + g)),
             pl.BlockSpec((1, V7X_LANES), lambda b, g: (0, 0)),
             pl.BlockSpec((1, V7X_LANES), lambda b, g: (0, 0)),
             pl.BlockSpec((1, SSD_GW), lambda b, g: (0, g))]
    cbr = conv_b.reshape(1, SSD_CONV_CH)
    ins = [proj, proj, proj, proj, conv_w, conv_w, conv_w, cbr, cbr, cbr,
           a_log.reshape(1, 2 * SSD_HEADS), dt_bias.reshape(1, 2 * SSD_HEADS),
           jnp.repeat(d_skip, SSD_P).reshape(1, SSD_DI)]
    state_spec = pl.BlockSpec((1, 2, SSD_GW, SSD_N), lambda b, g: (b, 0, g, 0))
    if latent:
        specs.append(state_spec)
        ins.append(h0)
    out_shape = [jax.ShapeDtypeStruct((T, SSD_DI), f32)]
    out_specs = [pl.BlockSpec((seq, SSD_GW), lambda b, g: (rb + b, g))]
    if not latent:
        out_shape.append(jax.ShapeDtypeStruct((nb, 2, SSD_HEADS * SSD_P, SSD_N), f32))
        out_specs.append(state_spec)
    pad_rows = seq + 16
    return _call_into(
        functools.partial(_ssd_kernel, seq=seq, has_h0=latent, want_state=not latent), prev, ins,
        grid=(nb, SSD_G), in_specs=specs, out_specs=out_specs, out_shape=out_shape,
        scratch_shapes=[pltpu.VMEM((pad_rows, SSD_GW), f32), pltpu.VMEM((pad_rows, SSD_N), f32),
                        pltpu.VMEM((pad_rows, SSD_N), f32),
                        pltpu.VMEM((seq, SSD_GW), f32), pltpu.VMEM((seq, SSD_N), f32), pltpu.VMEM((seq, SSD_N), f32),
                        pltpu.VMEM((SSD_GW, SSD_N), f32)],
        compiler_params=_cparams(("parallel", "arbitrary")),
        name="ssd_latent" if latent else "ssd_context")


def _axial_tables(half):
    t = np.arange(SS)
    n = half // 2
    inv = ROPE_BASE ** (-np.arange(n, dtype=np.float32) / n)
    cos, sin = [], []
    for pos in (t // GRID_W, t % GRID_W):
        ang = pos.astype(np.float32)[:, None] * inv[None].astype(np.float32)
        cos += [np.cos(ang), np.cos(ang)]
        sin += [-np.sin(ang), np.sin(ang)]
    return np.concatenate(cos, axis=1).astype(np.float32), np.concatenate(sin, axis=1).astype(np.float32)


def _token_tables(cos_lat, sin_lat):
    w = cos_lat.shape[1]
    cos = np.concatenate([np.ones((T_P, w), np.float32), np.tile(cos_lat, (BS, 1))], axis=0)
    sin = np.concatenate([np.zeros((T_P, w), np.float32), np.tile(sin_lat, (BS, 1))], axis=0)
    return jnp.asarray(cos), jnp.asarray(sin)


def _rope_tables():
    c128, s128 = _axial_tables(DIFF_HD // 2)
    c64, s64 = _axial_tables(MLA_ROPE // 2)
    one, zero = np.ones((SS, 64), np.float32), np.zeros((SS, 64), np.float32)
    one2, zero2 = np.ones((SS, 128), np.float32), np.zeros((SS, 128), np.float32)
    diff = _token_tables(c128, s128)
    mla_q = _token_tables(np.concatenate([one2, c64, one], 1), np.concatenate([zero2, s64, zero], 1))
    mla_kr = _token_tables(np.concatenate([c64, one], 1), np.concatenate([s64, zero], 1))
    return diff, mla_q, mla_kr


def kernel(x_prompt, x_sample, cache_diff_k, cache_diff_v, cache_na_k, cache_na_v, cache_mla_ckv, cache_mla_krope,
           state_ssd, c, c_ctx, ada_w, ada_b, norm_w, ffn_w_gu, ffn_w_down, final_norm_w, diff_w_qkv, diff_w_o,
           diff_lambda, diff_subln_w, na_w_qkv, na_w_o, na_rpb, mla_w_dq, mla_q_norm, mla_w_uq, mla_w_dkv,
           mla_kv_norm, mla_w_ukv, mla_w_o, ssd_w_in, ssd_conv_w, ssd_conv_b, ssd_a_log, ssd_dt_bias, ssd_d,
           ssd_norm_w, ssd_w_out):
    x = jnp.concatenate([x_prompt.reshape(T_P, D), x_sample.reshape(T_S, D)], axis=0)
    cond = jnp.concatenate([c_ctx[None], c, jnp.zeros((COND_PAD - N_COND, D), f32)], axis=0)
    mods = _modulation(cond, ada_w, ada_b)[:, :N_COND].reshape(DEPTH, N_COND, 3, 3, 1, D)
    rope_diff, rope_mla_q, rope_mla_kr = _rope_tables()

    def mod(i, k, what):
        return mods[i, :, k, what]

    def o_proj(a, w_o, i, **kw):
        return _fused_matmul(a, w_o.astype(bf16), epi="residual", epi_args=(x, mod(i, 1, 2)), tn=D, **kw)

    outs = {}
    for i in range(DEPTH):
        x = _ffn(x, norm_w[i, 0], mod(i, 0, 0), mod(i, 0, 1), mod(i, 0, 2), ffn_w_gu[i, 0].astype(bf16),
                 ffn_w_down[i, 0].astype(bf16), final_norm_w, final_norm=False)
        pre_args = (norm_w[i, 1], mod(i, 1, 0), mod(i, 1, 1))
        kind = i % 4
        if kind == 0:
            lam_init = 0.8 - 0.6 * math.exp(-0.3 * i)
            qkv = _fused_matmul(x, diff_w_qkv[0].astype(bf16), pre="mod", pre_args=pre_args, epi="rope",
                                epi_args=rope_diff, rope=(2 * D // QKV_TN, 0, QKV_TN, DIFF_HD // 4),
                                tm=QKV_TM, tn=QKV_TN, name="diff_qkv")
            outs["diff_k"] = qkv[:T_P, D:2 * D].reshape(BP, 1, SP, DIFF_HEADS, 2, DIFF_HD)
            outs["diff_v"] = qkv[:T_P, 2 * D:].reshape(BP, 1, SP, DIFF_HEADS, 2 * DIFF_HD)
            o = _diff_attention(qkv, diff_lambda[0], diff_subln_w[0], cache_diff_k.reshape(BS, PAST, D),
                                cache_diff_v.reshape(BS, PAST, D), lam_init=lam_init, latent=True)
            o = _diff_attention(qkv, diff_lambda[0], diff_subln_w[0], None, None, lam_init=lam_init, latent=False,
                                prev=o)
            x = o_proj(o, diff_w_o[0], i, name="diff_out")
        elif kind == 1:
            qkv = _fused_matmul(x, na_w_qkv[0].astype(bf16), pre="mod", pre_args=pre_args, tm=QKV_TM, tn=QKV_TN,
                                name="na_qkv")
            outs["na_k"] = qkv[:T_P, D:2 * D].reshape(BP, 1, SP, NA_HEADS, NA_HD)
            outs["na_v"] = qkv[:T_P, 2 * D:].reshape(BP, 1, SP, NA_HEADS, NA_HD)
            o = _na_latent_attention(qkv, cache_na_k.reshape(BS, PAST, D), cache_na_v.reshape(BS, PAST, D),
                                     _na_bias_tables(na_rpb[0]))
            o = _context_attention(qkv, o, heads=NA_HEADS, hd=NA_HD, scale=NA_HD ** -0.5)
            x = o_proj(o, na_w_o[0], i, name="na_out")
        elif kind == 2:
            w_lat = jnp.concatenate([mla_w_dq[0], mla_w_dkv[0],
                                     jnp.zeros((D, V7X_LANES - MLA_ROPE), f32)], axis=1).astype(bf16)
            n_lat = w_lat.shape[1]
            lat = _fused_matmul(x, w_lat, pre="mod", pre_args=pre_args, epi="rope", epi_args=rope_mla_kr,
                                rope=(1, n_lat - V7X_LANES, n_lat, MLA_ROPE // 4), tn=n_lat, name="mla_down")
            w_uq = jnp.pad(mla_w_uq[0].reshape(MLA_Q_LORA, MLA_HEADS, MLA_NOPE + MLA_ROPE),
                           ((0, 0), (0, 0), (0, MLA_QW - MLA_NOPE - MLA_ROPE))).reshape(MLA_Q_LORA, MLA_HEADS * MLA_QW)
            q = _fused_matmul(lat, w_uq.astype(bf16), pre="rms", pre_args=(mla_q_norm[0],), epi="rope",
                              epi_args=rope_mla_q, rope=(MLA_HEADS * MLA_QW // QKV_TN, 0, QKV_TN, MLA_ROPE // 4),
                              tm=QKV_TM, tn=QKV_TN, name="mla_q")
            w_ukv = mla_w_ukv[0].astype(bf16)
            kvx, ckv_n = _fused_matmul(lat, w_ukv, k_blk=MLA_Q_LORA // MLA_KV_LORA, pre="rms",
                                       pre_args=(mla_kv_norm[0],), want_pre_out=True, tm=QKV_TM, tn=QKV_TN,
                                       name="mla_kv")
            kvc = _fused_matmul(cache_mla_ckv.reshape(BS * PAST, MLA_KV_LORA), w_ukv, tm=QKV_TM, tn=QKV_TN,
                                name="mla_kv_cache")
            krc = jnp.pad(cache_mla_krope.reshape(BS * PAST, MLA_ROPE), ((0, 0), (0, V7X_LANES - MLA_ROPE)))
            outs["mla_ckv"] = ckv_n[:T_P].reshape(BP, 1, SP, MLA_KV_LORA)
            outs["mla_kr"] = lat[:T_P, MLA_Q_LORA + MLA_KV_LORA:MLA_Q_LORA + MLA_KV_LORA + MLA_ROPE].reshape(
                BP, 1, SP, MLA_ROPE)
            o = _mla_attention(q, kvx, lat, kvc, krc, latent=True)
            o = _mla_attention(q, kvx, lat, None, None, latent=False, prev=o)
            x = o_proj(o, mla_w_o[0], i, name="mla_out")
        else:
            proj = _fused_matmul(x, ssd_w_in[0].astype(bf16), pre="mod", pre_args=pre_args, tm=QKV_TM, tn=1152,
                                 name="ssd_in")
            args = (proj, ssd_conv_w[0], ssd_conv_b[0], ssd_a_log[0], ssd_dt_bias[0], ssd_d[0])
            (y,) = _ssd_scan(*args, state_ssd.reshape(BS, 2, SSD_HEADS * SSD_P, SSD_N), latent=True)
            y, h_t = _ssd_scan(*args, None, latent=False, prev=y)
            outs["ssd_h"] = h_t.reshape(BP, 1, 2, SSD_HEADS, SSD_P, SSD_N)
            x = o_proj(y, ssd_w_out[0], i, pre="gated_rms", pre_args=(proj, ssd_norm_w[0]), tm=TM // 2,
                       name="ssd_out")
        x = _ffn(x, norm_w[i, 2], mod(i, 2, 0), mod(i, 2, 1), mod(i, 2, 2), ffn_w_gu[i, 1].astype(bf16),
                 ffn_w_down[i, 1].astype(bf16), final_norm_w, final_norm=(i == DEPTH - 1))
    return (x[:T_P].reshape(BP, SP, D), x[T_P:].reshape(BS, SS, D), outs["diff_k"], outs["diff_v"], outs["na_k"],
            outs["na_v"], outs["mla_ckv"], outs["mla_kr"], outs["ssd_h"])
```

```python
import functools
import math

import numpy as np
import jax
import jax.numpy as jnp
from jax import lax
from jax.experimental import pallas as pl
from jax.experimental.pallas import tpu as pltpu

f32 = jnp.float32
bf16 = jnp.bfloat16

D = 2048
BP, SP = 16, 256
BS, SS = 8, 2048
PAST = 512
GRID_W = 64
GRID_R = SS // GRID_W
DEPTH = 4
D_FF = 5632
EPS = 1e-6
ROPE_BASE = 10000.0
T_P = BP * SP
T_S = BS * SS
T = T_P + T_S
N_COND = 1 + BS
COND_PAD = 16

DIFF_HEADS, DIFF_HD = 8, 128
NA_HEADS, NA_HD = 16, 128
NA_WIN_ROWS, NA_WIN_COLS = 8, 16
NA_QROWS = 4
NA_KROWS = 12
MLA_HEADS, MLA_Q_LORA, MLA_KV_LORA = 16, 512, 256
MLA_NOPE, MLA_ROPE, MLA_V = 128, 64, 128
MLA_QW = 256
SSD_DI = 2 * D
SSD_P, SSD_HEADS, SSD_G, SSD_HPG, SSD_N = 64, 64, 8, 8, 128
SSD_CONV_W, SSD_Q = 7, 128
SSD_CONV_CH = SSD_DI + 2 * SSD_G * SSD_N
SSD_IN = SSD_DI + SSD_CONV_CH + 2 * SSD_HEADS
SSD_GW = SSD_HPG * SSD_P

V7X_LANES = 128
V7X_VMEM_LIMIT = 60000 * 1024

TM = 512
QKV_TM, QKV_TN = 1024, 1024


def _cparams(sem):
    return pltpu.CompilerParams(dimension_semantics=sem, vmem_limit_bytes=V7X_VMEM_LIMIT)


def _mod_row(i, tm):
    start = i * tm
    return jnp.where(start < T_P, 0, 1 + (start - T_P) // SS)


def _silu(x):
    return x / (1.0 + jnp.exp(-x))


def _for_row_chunks(n_rows, fn, chunk=256):
    def body(c, carry):
        fn(pl.ds(pl.multiple_of(c * chunk, chunk), chunk))
        return carry

    lax.fori_loop(0, n_rows // chunk, body, 0)


def _call_into(kernel_fn, prev, ins, *, in_specs, **kw):
    if prev is None:
        return pl.pallas_call(kernel_fn, in_specs=in_specs, **kw)(*ins)
    n = len(ins)

    def body(*refs):
        kernel_fn(*refs[:n], *refs[n + 1:])

    return pl.pallas_call(body, in_specs=list(in_specs) + [pl.BlockSpec(memory_space=pl.ANY)],
                          input_output_aliases={n: 0}, **kw)(*ins, prev)


def _dot(a, b):
    return jnp.dot(a, b, preferred_element_type=f32)


def _dot_nt(a, b):
    return lax.dot_general(a, b, (((1,), (1,)), ((), ())), preferred_element_type=f32)


def _dot_tn(a, b):
    return lax.dot_general(a, b, (((0,), (0,)), ((), ())), preferred_element_type=f32)


def _dot_exact(a, b):
    return jnp.dot(a, b, preferred_element_type=f32, precision=lax.Precision.HIGHEST)


def _mod_kernel(c_ref, w_ref, b_ref, o_ref):
    s = _silu(c_ref[...])
    o_ref[0] = jnp.dot(s, w_ref[0], preferred_element_type=f32, precision=lax.Precision.HIGHEST) + b_ref[0]


def _modulation(cond, ada_w, ada_b):
    n = ada_w.shape[-1]
    tn = 1024
    return pl.pallas_call(
        _mod_kernel,
        grid=(DEPTH, n // tn),
        in_specs=[pl.BlockSpec((COND_PAD, D), lambda l, j: (0, 0)),
                  pl.BlockSpec((1, D, tn), lambda l, j: (l, 0, j)),
                  pl.BlockSpec((1, 1, tn), lambda l, j: (l, 0, j))],
        out_specs=pl.BlockSpec((1, COND_PAD, tn), lambda l, j: (l, 0, j)),
        out_shape=jax.ShapeDtypeStruct((DEPTH, COND_PAD, n), f32),
        compiler_params=_cparams(("arbitrary", "arbitrary")),
        name="modulation",
    )(cond, ada_w, ada_b.reshape(DEPTH, 1, n))


def _rope_apply(acc, cos, sin, shift):
    w = acc.shape[1]
    reps = w // cos.shape[1]
    if reps > 1:
        cos = jnp.concatenate([cos] * reps, axis=1)
        sin = jnp.concatenate([sin] * reps, axis=1)
    lane = lax.broadcasted_iota(jnp.int32, acc.shape, 1)
    first = (lane % (2 * shift)) < shift
    rot = jnp.where(first, pltpu.roll(acc, w - shift, axis=1), pltpu.roll(acc, shift, axis=1))
    return acc * cos + rot * sin


def _fused_matmul_kernel(*refs, pre, epi, rope, rope_all, want_pre_out, res_scale):
    refs = list(refs)
    a_ref = refs.pop(0)
    if pre == "mod":
        nw_ref, sh_ref, sc_ref = refs.pop(0), refs.pop(0), refs.pop(0)
    elif pre == "rms":
        nw_ref = refs.pop(0)
    elif pre == "gated_rms":
        z_ref, nw_ref = refs.pop(0), refs.pop(0)
    w_ref = refs.pop(0)
    if epi == "rope":
        cos_ref, sin_ref = refs.pop(0), refs.pop(0)
    elif epi == "residual":
        res_ref, gate_ref = refs.pop(0), refs.pop(0)
    o_ref = refs.pop(0)
    if want_pre_out:
        po_ref = refs.pop(0)
    h_ref = refs.pop(0)
    j = pl.program_id(1)

    @pl.when(j == 0)
    def _():
        def prologue(rows):
            a = a_ref[rows, :].astype(f32)
            if pre == "gated_rms":
                a = a * _silu(z_ref[rows, :])
            if pre != "none":
                a = a * lax.rsqrt(jnp.mean(a * a, axis=-1, keepdims=True) + EPS) * nw_ref[...]
            if pre == "mod":
                a = a * (1.0 + sc_ref[0]) + sh_ref[0]
            if want_pre_out:
                po_ref[rows, :] = a
            h_ref[rows, :] = a.astype(bf16)

        _for_row_chunks(a_ref.shape[0], prologue)

    acc = _dot(h_ref[...], w_ref[...])
    if epi == "rope":
        n_tiles, lo, hi, shift = rope
        if lo == 0 and hi == acc.shape[1]:
            roped = _rope_apply(acc, cos_ref[...], sin_ref[...], shift)
        else:
            mid = _rope_apply(acc[:, lo:hi], cos_ref[...], sin_ref[...], shift)
            roped = jnp.concatenate([acc[:, :lo], mid] + ([acc[:, hi:]] if hi < acc.shape[1] else []), axis=1)
        o_ref[...] = roped if rope_all else jnp.where(j < n_tiles, roped, acc)
    elif epi == "residual":
        o_ref[...] = res_ref[...] + (res_scale * gate_ref[0]) * acc
    else:
        o_ref[...] = acc.astype(o_ref.dtype)


def _fused_matmul(a, w, *, k_blk=0, pre="none", pre_args=(), epi="none", epi_args=(), rope=None,
                  want_pre_out=False, res_scale=1.0, tm=TM, tn=512, name="fused_matmul"):
    rows = a.shape[0]
    k, n = w.shape
    assert rows % tm == 0 and n % tn == 0
    grid = (rows // tm, n // tn)
    row_k = lambda i, j: (i, k_blk)
    vec_k = pl.BlockSpec((1, k), lambda i, j: (0, 0))
    mod_k = pl.BlockSpec((1, 1, k), lambda i, j: (_mod_row(i, tm), 0, 0))
    ins, specs = [a], [pl.BlockSpec((tm, k), row_k)]
    if pre == "mod":
        nw, sh, sc = pre_args
        ins += [nw.reshape(1, k), sh, sc]
        specs += [vec_k, mod_k, mod_k]
    elif pre == "rms":
        ins += [pre_args[0].reshape(1, k)]
        specs += [vec_k]
    elif pre == "gated_rms":
        z, nw = pre_args
        ins += [z, nw.reshape(1, k)]
        specs += [pl.BlockSpec((tm, k), lambda i, j: (i, 0)), vec_k]
    ins.append(w)
    if n == tn:
        specs.append(pl.BlockSpec((k, tn), lambda i, j: (0, 0), pipeline_mode=pl.Buffered(1)))
    else:
        specs.append(pl.BlockSpec((k, tn), lambda i, j: (0, j)))
    if epi == "rope":
        cos, sin = epi_args
        p = cos.shape[1]
        ins += [cos, sin]
        specs += [pl.BlockSpec((tm, p), lambda i, j: (i, 0))] * 2
    elif epi == "residual":
        res, gate = epi_args
        ins += [res, gate]
        specs += [pl.BlockSpec((tm, tn), lambda i, j: (i, j)),
                  pl.BlockSpec((1, 1, tn), lambda i, j: (_mod_row(i, tm), 0, j))]
    out_shape = [jax.ShapeDtypeStruct((rows, n), f32)]
    out_specs = [pl.BlockSpec((tm, tn), lambda i, j: (i, j))]
    if want_pre_out:
        out_shape.append(jax.ShapeDtypeStruct((rows, k), f32))
        out_specs.append(pl.BlockSpec((tm, k), lambda i, j: (i, 0)))
    outs = pl.pallas_call(
        functools.partial(_fused_matmul_kernel, pre=pre, epi=epi, rope=rope,
                          rope_all=rope is not None and rope[0] == n // tn, want_pre_out=want_pre_out,
                          res_scale=res_scale),
        grid=grid, in_specs=specs, out_specs=out_specs, out_shape=out_shape,
        scratch_shapes=[pltpu.VMEM((tm, k), bf16)],
        compiler_params=_cparams(("parallel", "arbitrary")),
        name=name,
    )(*ins)
    return outs if want_pre_out else outs[0]


def _ffn_kernel(x_ref, nw_ref, sh_ref, sc_ref, gate_ref, wg_ref, wu_ref, wd_ref, fw_ref, *refs, final_norm, split_at):
    if split_at is None:
        o_ref, h_ref, acc_ref = refs
    else:
        op_ref, os_ref, h_ref, acc_ref = refs
    f = pl.program_id(1)

    @pl.when(f == 0)
    def _():
        def prologue(rows):
            x = x_ref[rows, :]
            h = x * lax.rsqrt(jnp.mean(x * x, axis=-1, keepdims=True) + EPS) * nw_ref[...]
            h_ref[rows, :] = (h * (1.0 + sc_ref[0]) + sh_ref[0]).astype(bf16)

        _for_row_chunks(x_ref.shape[0], prologue)
        acc_ref[...] = jnp.zeros_like(acc_ref)

    h = h_ref[...]
    g = _dot(h, wg_ref[...])
    u = _dot(h, wu_ref[...])
    acc_ref[...] += _dot((_silu(g) * u).astype(bf16), wd_ref[...])

    @pl.when(f == pl.num_programs(1) - 1)
    def _():
        def epilogue(rows, o_ref):
            y = x_ref[rows, :] + (0.5 * gate_ref[0]) * acc_ref[rows, :]
            if final_norm:
                y = y * lax.rsqrt(jnp.mean(y * y, axis=-1, keepdims=True) + EPS) * fw_ref[...]
            o_ref[rows, :] = y

        if split_at is None:
            _for_row_chunks(x_ref.shape[0], functools.partial(epilogue, o_ref=o_ref))
        else:
            i = pl.program_id(0)

            @pl.when(i < split_at)
            def _():
                _for_row_chunks(x_ref.shape[0], functools.partial(epilogue, o_ref=op_ref))

            @pl.when(i >= split_at)
            def _():
                _for_row_chunks(x_ref.shape[0], functools.partial(epilogue, o_ref=os_ref))


def _ffn(x, nw, sh, sc, gate, wgu, wd, final_w, *, final_norm, split=False, tm=TM, tf=512):
    nf = D_FF // tf
    vec = pl.BlockSpec((1, D), lambda i, f: (0, 0))
    mod = pl.BlockSpec((1, 1, D), lambda i, f: (_mod_row(i, tm), 0, 0))
    if split:
        n_ctx = T_P // tm
        out_specs = [pl.BlockSpec((tm, D), lambda i, f: (jnp.minimum(i, n_ctx - 1), 0)),
                     pl.BlockSpec((tm, D), lambda i, f: (jnp.maximum(i - n_ctx, 0), 0))]
        out_shape = [jax.ShapeDtypeStruct((T_P, D), f32), jax.ShapeDtypeStruct((T_S, D), f32)]
    else:
        n_ctx = None
        out_specs = pl.BlockSpec((tm, D), lambda i, f: (i, 0))
        out_shape = jax.ShapeDtypeStruct((T, D), f32)
    return pl.pallas_call(
        functools.partial(_ffn_kernel, final_norm=final_norm, split_at=n_ctx),
        grid=(T // tm, nf),
        in_specs=[pl.BlockSpec((tm, D), lambda i, f: (i, 0)), vec, mod, mod, mod,
                  pl.BlockSpec((D, tf), lambda i, f: (0, f)),
                  pl.BlockSpec((D, tf), lambda i, f: (0, f + nf)),
                  pl.BlockSpec((tf, D), lambda i, f: (f, 0)),
                  vec],
        out_specs=out_specs,
        out_shape=out_shape,
        scratch_shapes=[pltpu.VMEM((tm, D), bf16), pltpu.VMEM((tm, D), f32)],
        compiler_params=_cparams(("arbitrary" if split else "parallel", "arbitrary")),
        name="ffn_final" if split else "ffn",
    )(x, nw.reshape(1, D), sh, sc, gate, wgu, wgu, wd, final_w.reshape(1, D))


LOG2E = math.log2(math.e)


def _exp2_weights(segs):
    m = functools.reduce(jnp.maximum, [jnp.max(s, axis=-1, keepdims=True) for s in segs])
    es = [jnp.exp2(s - m) for s in segs]
    l = functools.reduce(jnp.add, [jnp.sum(e, axis=-1, keepdims=True) for e in es])
    return es, 1.0 / l


def _diff_attn_kernel(*refs, scale, lam_init, has_cache, tq):
    if has_cache:
        lam_ref, sub_ref, q_ref, k_ref, v_ref, kc_ref, vc_ref, o_ref, kb_ref, vb_ref = refs
    else:
        lam_ref, sub_ref, q_ref, k_ref, v_ref, o_ref, kb_ref, vb_ref = refs
    seq = k_ref.shape[0]
    kb_ref[0:seq, :] = k_ref[...].astype(bf16)
    vb_ref[0:seq, :] = v_ref[...].astype(bf16)
    if has_cache:
        kb_ref[seq:seq + PAST, :] = kc_ref[0].astype(bf16)
        vb_ref[seq:seq + PAST, :] = vc_ref[0].astype(bf16)
    lp = lam_ref[...]
    lam = (jnp.exp(jnp.sum(lp[0:1] * lp[1:2], axis=-1, keepdims=True))
           - jnp.exp(jnp.sum(lp[2:3] * lp[3:4], axis=-1, keepdims=True)) + lam_init)
    sub = sub_ref[...]

    def tile(rows):
        q = q_ref[rows, :] * (scale * LOG2E)
        maps = []
        for m in range(2):
            sl = slice(m * DIFF_HD, (m + 1) * DIFF_HD)
            (e,), inv = _exp2_weights([_dot_nt(q[:, sl].astype(bf16), kb_ref[:, sl])])
            maps.append(_dot(e.astype(bf16), vb_ref[...]) * inv)
        o = maps[0] - lam * maps[1]
        o = o * lax.rsqrt(jnp.mean(o * o, axis=-1, keepdims=True) + EPS) * sub
        o_ref[rows, :] = o * (1.0 - lam_init)

    _for_row_chunks(seq, tile, chunk=tq)


def _diff_attention(qkv, lam_p, subln, cache_k, cache_v, *, lam_init, latent, prev=None, tq=256):
    hw = 2 * DIFF_HD
    scale = DIFF_HD ** -0.5
    if latent:
        nb, seq, row0 = BS, SS, T_P
    else:
        nb, seq, row0 = BP, SP, 0
    rb = row0 // seq
    nk = seq + (PAST if latent else 0)
    specs = [pl.BlockSpec((4, DIFF_HD), lambda b, h: (0, 0)), pl.BlockSpec((1, hw), lambda b, h: (0, 0)),
             pl.BlockSpec((seq, hw), lambda b, h: (rb + b, h)),
             pl.BlockSpec((seq, hw), lambda b, h: (rb + b, DIFF_HEADS + h)),
             pl.BlockSpec((seq, hw), lambda b, h: (rb + b, 2 * DIFF_HEADS + h))]
    ins = [lam_p, subln.reshape(1, hw), qkv, qkv, qkv]
    if latent:
        specs += [pl.BlockSpec((1, PAST, hw), lambda b, h: (b, 0, h))] * 2
        ins += [cache_k, cache_v]
    return _call_into(
        functools.partial(_diff_attn_kernel, scale=scale, lam_init=lam_init, has_cache=latent, tq=tq), prev, ins,
        grid=(nb, DIFF_HEADS), in_specs=specs,
        out_specs=pl.BlockSpec((seq, hw), lambda b, h: (rb + b, h)),
        out_shape=jax.ShapeDtypeStruct((T, D), f32),
        scratch_shapes=[pltpu.VMEM((nk, hw), bf16), pltpu.VMEM((nk, hw), bf16)],
        compiler_params=_cparams(("parallel", "parallel")),
        name="diff_attn_latent" if latent else "diff_attn_context")


def _attn_kernel(q_ref, k_ref, v_ref, o_ref, *, scale):
    q = (q_ref[...] * (scale * LOG2E)).astype(bf16)
    (e,), inv = _exp2_weights([_dot_nt(q, k_ref[...].astype(bf16))])
    o_ref[...] = _dot(e.astype(bf16), v_ref[...].astype(bf16)) * inv


def _context_attention(qkv, prev, *, heads, hd, scale):
    return _call_into(
        functools.partial(_attn_kernel, scale=scale), prev, [qkv, qkv, qkv],
        grid=(BP, heads),
        in_specs=[pl.BlockSpec((SP, hd), lambda b, h: (b, h)),
                  pl.BlockSpec((SP, hd), lambda b, h: (b, heads + h)),
                  pl.BlockSpec((SP, hd), lambda b, h: (b, 2 * heads + h))],
        out_specs=pl.BlockSpec((SP, hd), lambda b, h: (b, h)),
        out_shape=jax.ShapeDtypeStruct((T, heads * hd), f32),
        compiler_params=_cparams(("parallel", "parallel")),
        name="context_attn")


def _mla_attn_kernel(*refs, scale, has_cache, tq):
    if has_cache:
        q_ref, kn_ref, kr_ref, v_ref, knc_ref, krc_ref, vc_ref, o_ref, kb_ref, vb_ref = refs
    else:
        q_ref, kn_ref, kr_ref, v_ref, o_ref, kb_ref, vb_ref = refs
    seq = kn_ref.shape[0]
    kb_ref[0:seq, 0:MLA_NOPE] = kn_ref[...].astype(bf16)
    kb_ref[0:seq, MLA_NOPE:] = kr_ref[...].astype(bf16)
    vb_ref[0:seq, :] = v_ref[...].astype(bf16)
    if has_cache:
        kb_ref[seq:seq + PAST, 0:MLA_NOPE] = knc_ref[...].astype(bf16)
        kb_ref[seq:seq + PAST, MLA_NOPE:] = krc_ref[...].astype(bf16)
        vb_ref[seq:seq + PAST, :] = vc_ref[...].astype(bf16)

    def tile(rows):
        q = (q_ref[rows, :] * (scale * LOG2E)).astype(bf16)
        (e,), inv = _exp2_weights([_dot_nt(q, kb_ref[...])])
        o_ref[rows, :] = _dot(e.astype(bf16), vb_ref[...]) * inv

    _for_row_chunks(seq, tile, chunk=tq)


def _mla_attention(q, kvx, lat, kvc, krc, *, latent, prev=None, tq=256):
    scale = (MLA_NOPE + MLA_ROPE) ** -0.5
    kr_blk = (MLA_Q_LORA + MLA_KV_LORA) // V7X_LANES
    if latent:
        nb, seq, row0 = BS, SS, T_P
    else:
        nb, seq, row0 = BP, SP, 0
    rb = row0 // seq
    nk = seq + (PAST if latent else 0)
    specs = [pl.BlockSpec((seq, MLA_QW), lambda b, h: (rb + b, h)),
             pl.BlockSpec((seq, MLA_NOPE), lambda b, h: (rb + b, 2 * h)),
             pl.BlockSpec((seq, V7X_LANES), lambda b, h: (rb + b, kr_blk)),
             pl.BlockSpec((seq, MLA_V), lambda b, h: (rb + b, 2 * h + 1))]
    ins = [q, kvx, lat, kvx]
    if latent:
        specs += [pl.BlockSpec((PAST, MLA_NOPE), lambda b, h: (b, 2 * h)),
                  pl.BlockSpec((PAST, V7X_LANES), lambda b, h: (b, 0)),
                  pl.BlockSpec((PAST, MLA_V), lambda b, h: (b, 2 * h + 1))]
        ins += [kvc, krc, kvc]
    return _call_into(
        functools.partial(_mla_attn_kernel, scale=scale, has_cache=latent, tq=tq), prev, ins,
        grid=(nb, MLA_HEADS), in_specs=specs,
        out_specs=pl.BlockSpec((seq, MLA_V), lambda b, h: (rb + b, h)),
        out_shape=jax.ShapeDtypeStruct((T, MLA_HEADS * MLA_V), f32),
        scratch_shapes=[pltpu.VMEM((nk, MLA_QW), bf16), pltpu.VMEM((nk, MLA_V), bf16)],
        compiler_params=_cparams(("parallel", "parallel")),
        name="mla_attn_latent" if latent else "mla_attn_context")


def _na_block_plan():
    plan = []
    for r0 in range(0, GRID_R, NA_QROWS):
        kb = min(max(r0 - NA_WIN_ROWS // 2, 0), GRID_R - NA_KROWS)
        var = 0 if r0 == 0 else (2 if r0 == GRID_R - NA_QROWS else 1)
        plan.append((r0, kb, var))
    return plan


def _na_bias_tables(rpb):
    reps = {var: (r0, kb) for r0, kb, var in _na_block_plan()}
    nr, ncol = 2 * NA_WIN_ROWS - 1, 2 * NA_WIN_COLS - 1
    rsel = np.zeros((3, NA_QROWS, NA_KROWS, nr), np.float32)
    for var in range(3):
        r0, kb = reps[var]
        for i in range(NA_QROWS):
            r = r0 + i
            rs = min(max(r - NA_WIN_ROWS // 2, 0), GRID_R - NA_WIN_ROWS)
            for kk in range(NA_KROWS):
                kr = kb + kk
                if rs <= kr < rs + NA_WIN_ROWS:
                    rsel[var, i, kk, kr - r + NA_WIN_ROWS - 1] = 1.0
    csel = np.zeros((GRID_W, GRID_W, ncol), np.float32)
    for c in range(GRID_W):
        cstart = min(max(c - NA_WIN_COLS // 2, 0), GRID_W - NA_WIN_COLS)
        for kc in range(cstart, cstart + NA_WIN_COLS):
            csel[c, kc, kc - c + NA_WIN_COLS - 1] = 1.0
    ok = (rsel.sum(-1) > 0)[:, None, :, None, :, None] & (csel.sum(-1) > 0)[None, None, None, :, None, :]
    val = jnp.einsum("vikr,hrd->vhikd", rsel, rpb, precision=lax.Precision.HIGHEST)
    tab = jnp.einsum("vhikd,cjd->vhickj", val, csel, precision=lax.Precision.HIGHEST)
    tab = jnp.where(ok, tab * LOG2E, -1e30)
    return tab.reshape(3, NA_HEADS, NA_QROWS * GRID_W, NA_KROWS * GRID_W).astype(f32)


def _na_kernel(q_ref, k_ref, v_ref, ck_ref, cv_ref, bias_ref, o_ref, *, scale):
    ck = ck_ref[0].astype(bf16)
    cv = cv_ref[0].astype(bf16)
    nq = NA_QROWS * GRID_W
    nk = NA_KROWS * GRID_W
    for r0, kb, var in _na_block_plan():
        q = (q_ref[r0 * GRID_W:r0 * GRID_W + nq, :] * (scale * LOG2E)).astype(bf16)
        k = k_ref[kb * GRID_W:kb * GRID_W + nk, :].astype(bf16)
        v = v_ref[kb * GRID_W:kb * GRID_W + nk, :].astype(bf16)
        (e_loc, e_ctx), inv = _exp2_weights([_dot_nt(q, k) + bias_ref[var, 0], _dot_nt(q, ck)])
        o_ref[r0 * GRID_W:r0 * GRID_W + nq, :] = (_dot(e_loc.astype(bf16), v) + _dot(e_ctx.astype(bf16), cv)) * inv


def _na_latent_attention(qkv, cache_k, cache_v, bias):
    scale = NA_HD ** -0.5
    rb = T_P // SS
    nk = NA_KROWS * GRID_W
    return pl.pallas_call(
        functools.partial(_na_kernel, scale=scale),
        grid=(NA_HEADS, BS),
        in_specs=[pl.BlockSpec((SS, NA_HD), lambda h, b: (rb + b, h)),
                  pl.BlockSpec((SS, NA_HD), lambda h, b: (rb + b, NA_HEADS + h)),
                  pl.BlockSpec((SS, NA_HD), lambda h, b: (rb + b, 2 * NA_HEADS + h)),
                  pl.BlockSpec((1, PAST, NA_HD), lambda h, b: (b, 0, h)),
                  pl.BlockSpec((1, PAST, NA_HD), lambda h, b: (b, 0, h)),
                  pl.BlockSpec((3, 1, NA_QROWS * GRID_W, nk), lambda h, b: (0, h, 0, 0))],
        out_specs=pl.BlockSpec((SS, NA_HD), lambda h, b: (rb + b, h)),
        out_shape=jax.ShapeDtypeStruct((T, D), f32),
        compiler_params=_cparams(("parallel", "arbitrary")),
        name="na_attn_latent",
    )(qkv, qkv, qkv, cache_k, cache_v, bias)


SSD_PAIRS = SSD_HPG // 2
SSD_DLANES = 16


def _ssd_kernel(*refs, seq, has_h0, want_state):
    refs = list(refs)
    (x_ref, b_ref, c_ref, dt_ref, cwx_ref, cwb_ref, cwc_ref, cbx_ref, cbb_ref, cbc_ref,
     alog_ref, dtb_ref, dsk_ref) = refs[:13]
    refs = refs[13:]
    h0_ref = refs.pop(0) if has_h0 else None
    y_ref = refs.pop(0)
    hT_ref = refs.pop(0) if want_state else None
    xp_ref, bp_ref, cp_ref, xs_ref, bt_ref, cs_ref, ac_ref, act_ref, dtt_ref, h_ref = refs
    nc = seq // SSD_Q
    halo = 8

    for src, pad, dst, cw_ref, cb_ref, transpose in ((x_ref, xp_ref, xs_ref, cwx_ref, cbx_ref, False),
                                                     (b_ref, bp_ref, bt_ref, cwb_ref, cbb_ref, True),
                                                     (c_ref, cp_ref, cs_ref, cwc_ref, cbc_ref, False)):
        width = src.shape[1]
        pad[0:halo, :] = jnp.zeros((halo, width), f32)
        pad[halo + seq:2 * halo + seq, :] = jnp.zeros((halo, width), f32)
        pad[halo:halo + seq, :] = src[...]
        cw = cw_ref[...]
        cb = cb_ref[...]

        def conv_chunk(c, carry, pad=pad, dst=dst, cw=cw, cb=cb, width=width, transpose=transpose):
            base = pl.multiple_of(c * SSD_Q, SSD_Q)
            win = pad[pl.ds(base, SSD_Q + 2 * halo), :]
            acc = jnp.broadcast_to(cb, (SSD_Q, width))
            for t in range(SSD_CONV_W):
                off = halo - SSD_CONV_W // 2 + t
                sh = pltpu.roll(win, SSD_Q + 2 * halo - off, axis=0)[:SSD_Q]
                acc = acc + sh * cw[t:t + 1, :]
            out = _silu(acc)
            dst[pl.ds(base, SSD_Q), :] = out.T if transpose else out
            return carry

        lax.fori_loop(0, nc, conv_chunk, 0)

    row = lax.broadcasted_iota(jnp.int32, (SSD_Q, SSD_Q), 0)
    col = lax.broadcasted_iota(jnp.int32, (SSD_Q, SSD_Q), 1)
    tri = {0: (col <= row), 1: (col >= row)}
    low_half = col < SSD_P
    a_neg = -jnp.exp(alog_ref[0])
    dtb = dtb_ref[0]
    dsk = dsk_ref[...]
    tril = tri[0].astype(f32)

    def dt_chunk(c, carry):
        base = pl.multiple_of(c * SSD_Q, SSD_Q)
        dt = dt_ref[pl.ds(base, SSD_Q), :] + dtb
        dt = jnp.maximum(dt, 0.0) + jnp.log1p(jnp.exp(-jnp.abs(dt)))
        a = dt * a_neg
        pre = _dot_exact(tril, a)
        suf = pre[SSD_Q - 1:SSD_Q, :] - pre + a
        acum = jnp.where(col < SSD_HPG, pre, suf)
        ac_ref[pl.ds(base, SSD_Q), :] = acum
        tb = pl.multiple_of(c * SSD_DLANES, SSD_DLANES)
        act_ref[pl.ds(tb, SSD_DLANES), :] = acum.T[:SSD_DLANES]
        dtt_ref[pl.ds(tb, SSD_DLANES), :] = dt.T[:SSD_DLANES]
        return carry

    lax.fori_loop(0, nc, dt_chunk, 0)

    for direction in range(2):
        mask = tri[direction]
        o = direction * SSD_HPG
        for k in range(SSD_PAIRS):
            if has_h0:
                h_ref[k] = h0_ref[0, direction, k * SSD_Q:(k + 1) * SSD_Q, :].T
            else:
                h_ref[k] = jnp.zeros((SSD_N, 2 * SSD_P), f32)

        def chunk(ci, carry, direction=direction, mask=mask, o=o):
            c = ci if direction == 0 else nc - 1 - ci
            base = pl.multiple_of(c * SSD_Q, SSD_Q)
            rows = pl.ds(base, SSD_Q)
            tb = pl.multiple_of(c * SSD_DLANES, SSD_DLANES)
            xc = xs_ref[rows, :]
            cc = cs_ref[rows, :]
            bt = bt_ref[rows, :]
            acum = ac_ref[rows, :]
            acum_t = act_ref[pl.ds(tb, SSD_DLANES), :]
            dt_t = dtt_ref[pl.ds(tb, SSD_DLANES), :]
            tot = acum_t[:, SSD_Q - 1:SSD_Q] if direction == 0 else acum_t[:, 0:1]
            w_t = dt_t * jnp.exp(tot - acum_t)
            decay = jnp.exp(tot)
            cb = _dot(cc.astype(bf16), bt.astype(bf16))
            ys = []
            for k in range(SSD_PAIRS):
                xk = xc[:, k * 2 * SSD_P:(k + 1) * 2 * SSD_P].astype(bf16)
                hk = h_ref[k]
                y_pair = None
                st_pair = None
                for e in range(2):
                    j = o + 2 * k + e
                    half = low_half if e == 0 else jnp.logical_not(low_half)
                    a_q = jnp.broadcast_to(acum[:, j:j + 1], (SSD_Q, SSD_Q))
                    seg = a_q - acum_t[j:j + 1, :]
                    lmat = jnp.where(mask, jnp.exp(jnp.where(mask, seg, 0.0)), 0.0)
                    lhs = jnp.concatenate([cb * lmat * dt_t[j:j + 1, :], cc * jnp.exp(a_q)], axis=1).astype(bf16)
                    x_e = jnp.where(half, xk, jnp.zeros_like(xk))
                    h_e = jnp.where(half, hk, 0.0).astype(bf16)
                    part = _dot(lhs, jnp.concatenate([x_e, h_e], axis=0))
                    st = _dot((bt * w_t[j:j + 1, :]).astype(bf16), x_e)
                    y_pair = part if y_pair is None else y_pair + part
                    st_pair = st if st_pair is None else st_pair + st
                d_row = jnp.where(low_half[0:1], decay[o + 2 * k:o + 2 * k + 1, :], decay[o + 2 * k + 1:o + 2 * k + 2, :])
                h_ref[k] = hk * d_row + st_pair
                ys.append(y_pair)
            y = jnp.concatenate(ys, axis=1)
            if direction == 0:
                y_ref[rows, :] = y + xc * dsk
            else:
                y_ref[rows, :] += y
            return carry

        lax.fori_loop(0, nc, chunk, 0)
        if want_state:
            for k in range(SSD_PAIRS):
                hT_ref[0, direction, k * SSD_Q:(k + 1) * SSD_Q, :] = h_ref[k].T


def _ssd_group_dt_columns(m):
    rows = m.shape[0]
    fb = m.reshape(rows, 2, SSD_G, SSD_HPG).transpose(0, 2, 1, 3).reshape(rows, SSD_G, SSD_DLANES)
    return jnp.pad(fb, ((0, 0), (0, 0), (0, V7X_LANES - SSD_DLANES))).reshape(rows, SSD_G * V7X_LANES)


def _ssd_scan(proj, conv_w, conv_b, a_log, dt_bias, d_skip, h0, *, latent, prev=None):
    if latent:
        nb, seq, row0 = BS, SS, T_P
    else:
        nb, seq, row0 = BP, SP, 0
    rb = row0 // seq
    xb0 = SSD_DI // SSD_GW
    bb0 = 2 * SSD_DI // SSD_N
    cb0 = bb0 + SSD_G
    dtb0 = (SSD_DI + SSD_CONV_CH) // V7X_LANES
    cwb0 = SSD_DI // SSD_N
    row = lambda blk: (lambda b, g: (rb + b, blk(g)))
    grp = pl.BlockSpec((1, 1, V7X_LANES), lambda b, g: (g, 0, 0))
    specs = [pl.BlockSpec((seq, SSD_GW), row(lambda g: xb0 + g)),
             pl.BlockSpec((seq, SSD_N), row(lambda g: bb0 + g)),
             pl.BlockSpec((seq, SSD_N), row(lambda g: cb0 + g)),
             pl.BlockSpec((seq, V7X_LANES), row(lambda g: dtb0 + g)),
             pl.BlockSpec((SSD_CONV_W, SSD_GW), lambda b, g: (0, g)),
             pl.BlockSpec((SSD_CONV_W, SSD_N), lambda b, g: (0, cwb0 + g)),
             pl.BlockSpec((SSD_CONV_W, SSD_N), lambda b, g: (0, cwb0 + SSD_G + g)),
             pl.BlockSpec((1, SSD_GW), lambda b, g: (0, g)),
             pl.BlockSpec((1, SSD_N), lambda b, g: (0, cwb0 + g)),
             pl.BlockSpec((1, SSD_N), lambda b, g: (0, cwb0 + SSD_G + g)),
             grp, grp,
             pl.BlockSpec((1, SSD_GW), lambda b, g: (0, g))]
    cbr = conv_b.reshape(1, SSD_CONV_CH)
    ins = [proj, proj, proj, proj, conv_w, conv_w, conv_w, cbr, cbr, cbr,
           _ssd_group_dt_columns(a_log.reshape(1, 2 * SSD_HEADS)).reshape(SSD_G, 1, V7X_LANES),
           _ssd_group_dt_columns(dt_bias.reshape(1, 2 * SSD_HEADS)).reshape(SSD_G, 1, V7X_LANES),
           jnp.repeat(d_skip, SSD_P).reshape(1, SSD_DI)]
    state_spec = pl.BlockSpec((1, 2, SSD_GW, SSD_N), lambda b, g: (b, 0, g, 0))
    if latent:
        specs.append(state_spec)
        ins.append(h0)
    out_shape = [jax.ShapeDtypeStruct((T, SSD_DI), f32)]
    out_specs = [pl.BlockSpec((seq, SSD_GW), lambda b, g: (rb + b, g))]
    if not latent:
        out_shape.append(jax.ShapeDtypeStruct((nb, 2, SSD_HEADS * SSD_P, SSD_N), f32))
        out_specs.append(state_spec)
    pad_rows = seq + 16
    return _call_into(
        functools.partial(_ssd_kernel, seq=seq, has_h0=latent, want_state=not latent), prev, ins,
        grid=(nb, SSD_G), in_specs=specs, out_specs=out_specs, out_shape=out_shape,
        scratch_shapes=[pltpu.VMEM((pad_rows, SSD_GW), f32), pltpu.VMEM((pad_rows, SSD_N), f32),
                        pltpu.VMEM((pad_rows, SSD_N), f32),
                        pltpu.VMEM((seq, SSD_GW), f32), pltpu.VMEM((seq, SSD_Q), f32), pltpu.VMEM((seq, SSD_N), f32),
                        pltpu.VMEM((seq, V7X_LANES), f32),
                        pltpu.VMEM((seq // SSD_Q * SSD_DLANES, SSD_Q), f32),
                        pltpu.VMEM((seq // SSD_Q * SSD_DLANES, SSD_Q), f32),
                        pltpu.VMEM((SSD_PAIRS, SSD_N, 2 * SSD_P), f32)],
        compiler_params=_cparams(("parallel", "arbitrary")),
        name="ssd_latent" if latent else "ssd_context")


def _axial_tables(half):
    t = np.arange(SS)
    n = half // 2
    inv = ROPE_BASE ** (-np.arange(n, dtype=np.float32) / n)
    cos, sin = [], []
    for pos in (t // GRID_W, t % GRID_W):
        ang = pos.astype(np.float32)[:, None] * inv[None].astype(np.float32)
        cos += [np.cos(ang), np.cos(ang)]
        sin += [-np.sin(ang), np.sin(ang)]
    return np.concatenate(cos, axis=1).astype(np.float32), np.concatenate(sin, axis=1).astype(np.float32)


def _token_tables(cos_lat, sin_lat):
    w = cos_lat.shape[1]
    cos = np.concatenate([np.ones((T_P, w), np.float32), np.tile(cos_lat, (BS, 1))], axis=0)
    sin = np.concatenate([np.zeros((T_P, w), np.float32), np.tile(sin_lat, (BS, 1))], axis=0)
    return jnp.asarray(cos), jnp.asarray(sin)


def _rope_tables():
    c128, s128 = _axial_tables(DIFF_HD // 2)
    c64, s64 = _axial_tables(MLA_ROPE // 2)
    one, zero = np.ones((SS, 64), np.float32), np.zeros((SS, 64), np.float32)
    one2, zero2 = np.ones((SS, 128), np.float32), np.zeros((SS, 128), np.float32)
    diff = _token_tables(c128, s128)
    mla_q = _token_tables(np.concatenate([one2, c64, one], 1), np.concatenate([zero2, s64, zero], 1))
    mla_kr = _token_tables(np.concatenate([c64, one], 1), np.concatenate([s64, zero], 1))
    return diff, mla_q, mla_kr


def kernel(x_prompt, x_sample, cache_diff_k, cache_diff_v, cache_na_k, cache_na_v, cache_mla_ckv, cache_mla_krope,
           state_ssd, c, c_ctx, ada_w, ada_b, norm_w, ffn_w_gu, ffn_w_down, final_norm_w, diff_w_qkv, diff_w_o,
           diff_lambda, diff_subln_w, na_w_qkv, na_w_o, na_rpb, mla_w_dq, mla_q_norm, mla_w_uq, mla_w_dkv,
           mla_kv_norm, mla_w_ukv, mla_w_o, ssd_w_in, ssd_conv_w, ssd_conv_b, ssd_a_log, ssd_dt_bias, ssd_d,
           ssd_norm_w, ssd_w_out):
    x = jnp.concatenate([x_prompt.reshape(T_P, D), x_sample.reshape(T_S, D)], axis=0)
    cond = jnp.concatenate([c_ctx[None], c, jnp.zeros((COND_PAD - N_COND, D), f32)], axis=0)
    mods = _modulation(cond, ada_w, ada_b)[:, :N_COND].reshape(DEPTH, N_COND, 3, 3, 1, D)
    rope_diff, rope_mla_q, rope_mla_kr = _rope_tables()

    def mod(i, k, what):
        return mods[i, :, k, what]

    def o_proj(a, w_o, i, **kw):
        return _fused_matmul(a, w_o.astype(bf16), epi="residual", epi_args=(x, mod(i, 1, 2)), tn=D, **kw)

    outs = {}
    for i in range(DEPTH):
        x = _ffn(x, norm_w[i, 0], mod(i, 0, 0), mod(i, 0, 1), mod(i, 0, 2), ffn_w_gu[i, 0].astype(bf16),
                 ffn_w_down[i, 0].astype(bf16), final_norm_w, final_norm=False)
        pre_args = (norm_w[i, 1], mod(i, 1, 0), mod(i, 1, 1))
        kind = i % 4
        if kind == 0:
            lam_init = 0.8 - 0.6 * math.exp(-0.3 * i)
            qkv = _fused_matmul(x, diff_w_qkv[0].astype(bf16), pre="mod", pre_args=pre_args, epi="rope",
                                epi_args=rope_diff, rope=(2 * D // QKV_TN, 0, QKV_TN, DIFF_HD // 4),
                                tm=QKV_TM, tn=QKV_TN, name="diff_qkv")
            outs["diff_k"] = qkv[:T_P, D:2 * D].reshape(BP, 1, SP, DIFF_HEADS, 2, DIFF_HD)
            outs["diff_v"] = qkv[:T_P, 2 * D:].reshape(BP, 1, SP, DIFF_HEADS, 2 * DIFF_HD)
            o = _diff_attention(qkv, diff_lambda[0], diff_subln_w[0], cache_diff_k.reshape(BS, PAST, D),
                                cache_diff_v.reshape(BS, PAST, D), lam_init=lam_init, latent=True)
            o = _diff_attention(qkv, diff_lambda[0], diff_subln_w[0], None, None, lam_init=lam_init, latent=False,
                                prev=o)
            x = o_proj(o, diff_w_o[0], i, name="diff_out")
        elif kind == 1:
            qkv = _fused_matmul(x, na_w_qkv[0].astype(bf16), pre="mod", pre_args=pre_args, tm=QKV_TM, tn=QKV_TN,
                                name="na_qkv")
            outs["na_k"] = qkv[:T_P, D:2 * D].reshape(BP, 1, SP, NA_HEADS, NA_HD)
            outs["na_v"] = qkv[:T_P, 2 * D:].reshape(BP, 1, SP, NA_HEADS, NA_HD)
            o = _na_latent_attention(qkv, cache_na_k.reshape(BS, PAST, D), cache_na_v.reshape(BS, PAST, D),
                                     _na_bias_tables(na_rpb[0]))
            o = _context_attention(qkv, o, heads=NA_HEADS, hd=NA_HD, scale=NA_HD ** -0.5)
            x = o_proj(o, na_w_o[0], i, name="na_out")
        elif kind == 2:
            w_lat = jnp.concatenate([mla_w_dq[0], mla_w_dkv[0],
                                     jnp.zeros((D, V7X_LANES - MLA_ROPE), f32)], axis=1).astype(bf16)
            n_lat = w_lat.shape[1]
            lat = _fused_matmul(x, w_lat, pre="mod", pre_args=pre_args, epi="rope", epi_args=rope_mla_kr,
                                rope=(1, n_lat - V7X_LANES, n_lat, MLA_ROPE // 4), tn=n_lat, name="mla_down")
            w_uq = jnp.pad(mla_w_uq[0].reshape(MLA_Q_LORA, MLA_HEADS, MLA_NOPE + MLA_ROPE),
                           ((0, 0), (0, 0), (0, MLA_QW - MLA_NOPE - MLA_ROPE))).reshape(MLA_Q_LORA, MLA_HEADS * MLA_QW)
            q = _fused_matmul(lat, w_uq.astype(bf16), pre="rms", pre_args=(mla_q_norm[0],), epi="rope",
                              epi_args=rope_mla_q, rope=(MLA_HEADS * MLA_QW // QKV_TN, 0, QKV_TN, MLA_ROPE // 4),
                              tm=QKV_TM, tn=QKV_TN, name="mla_q")
            w_ukv = mla_w_ukv[0].astype(bf16)
            kvx, ckv_n = _fused_matmul(lat, w_ukv, k_blk=MLA_Q_LORA // MLA_KV_LORA, pre="rms",
                                       pre_args=(mla_kv_norm[0],), want_pre_out=True, tm=QKV_TM, tn=QKV_TN,
                                       name="mla_kv")
            kvc = _fused_matmul(cache_mla_ckv.reshape(BS * PAST, MLA_KV_LORA), w_ukv, tm=QKV_TM, tn=QKV_TN,
                                name="mla_kv_cache")
            krc = jnp.pad(cache_mla_krope.reshape(BS * PAST, MLA_ROPE), ((0, 0), (0, V7X_LANES - MLA_ROPE)))
            outs["mla_ckv"] = ckv_n[:T_P].reshape(BP, 1, SP, MLA_KV_LORA)
            outs["mla_kr"] = lat[:T_P, MLA_Q_LORA + MLA_KV_LORA:MLA_Q_LORA + MLA_KV_LORA + MLA_ROPE].reshape(
                BP, 1, SP, MLA_ROPE)
            o = _mla_attention(q, kvx, lat, kvc, krc, latent=True)
            o = _mla_attention(q, kvx, lat, None, None, latent=False, prev=o)
            x = o_proj(o, mla_w_o[0], i, name="mla_out")
        else:
            n_xz = SSD_DI + SSD_CONV_CH
            w_in = jnp.concatenate([ssd_w_in[0][:, :n_xz], _ssd_group_dt_columns(ssd_w_in[0][:, n_xz:])], axis=1)
            proj = _fused_matmul(x, w_in.astype(bf16), pre="mod", pre_args=pre_args, tm=QKV_TM, tn=QKV_TN,
                                 name="ssd_in")
            args = (proj, ssd_conv_w[0], ssd_conv_b[0], ssd_a_log[0], ssd_dt_bias[0], ssd_d[0])
            (y,) = _ssd_scan(*args, state_ssd.reshape(BS, 2, SSD_HEADS * SSD_P, SSD_N), latent=True)
            y, h_t = _ssd_scan(*args, None, latent=False, prev=y)
            outs["ssd_h"] = h_t.reshape(BP, 1, 2, SSD_HEADS, SSD_P, SSD_N)
            x = o_proj(y, ssd_w_out[0], i, pre="gated_rms", pre_args=(proj, ssd_norm_w[0]), tm=TM // 2,
                       name="ssd_out")
        last = i == DEPTH - 1
        x = _ffn(x, norm_w[i, 2], mod(i, 2, 0), mod(i, 2, 1), mod(i, 2, 2), ffn_w_gu[i, 1].astype(bf16),
                 ffn_w_down[i, 1].astype(bf16), final_norm_w, final_norm=last, split=last)
    y_ctx, y_lat = x
    return (y_ctx.reshape(BP, SP, D), y_lat.reshape(BS, SS, D), outs["diff_k"], outs["diff_v"], outs["na_k"],
            outs["na_v"], outs["mla_ckv"], outs["mla_kr"], outs["ssd_h"])
```

```python
import functools
import math

import numpy as np
import jax
import jax.numpy as jnp
from jax import lax
from jax.experimental import pallas as pl
from jax.experimental.pallas import tpu as pltpu

f32 = jnp.float32
bf16 = jnp.bfloat16

D = 2048
BP, SP = 16, 256
BS, SS = 8, 2048
PAST = 512
GRID_W = 64
GRID_R = SS // GRID_W
DEPTH = 4
D_FF = 5632
EPS = 1e-6
ROPE_BASE = 10000.0
T_P = BP * SP
T_S = BS * SS
T = T_P + T_S
N_COND = 1 + BS
COND_PAD = 16

DIFF_HEADS, DIFF_HD = 8, 128
NA_HEADS, NA_HD = 16, 128
NA_WIN_ROWS, NA_WIN_COLS = 8, 16
NA_QROWS = 4
NA_KROWS = 12
MLA_HEADS, MLA_Q_LORA, MLA_KV_LORA = 16, 512, 256
MLA_NOPE, MLA_ROPE, MLA_V = 128, 64, 128
MLA_QW = 256
SSD_DI = 2 * D
SSD_P, SSD_HEADS, SSD_G, SSD_HPG, SSD_N = 64, 64, 8, 8, 128
SSD_CONV_W, SSD_Q = 7, 128
SSD_CONV_CH = SSD_DI + 2 * SSD_G * SSD_N
SSD_IN = SSD_DI + SSD_CONV_CH + 2 * SSD_HEADS
SSD_GW = SSD_HPG * SSD_P

V7X_LANES = 128
V7X_VMEM_LIMIT = 60000 * 1024

TM = 512
QKV_TM, QKV_TN = 1024, 1024
DIFF_TQ, MLA_TQ = 512, 256
ROW_CHUNK = 256


def _cparams(sem):
    return pltpu.CompilerParams(dimension_semantics=sem, vmem_limit_bytes=V7X_VMEM_LIMIT)


def _mod_row(i, tm):
    start = i * tm
    return jnp.where(start < T_P, 0, 1 + (start - T_P) // SS)


def _silu(x):
    return x / (1.0 + jnp.exp(-x))


def _for_row_chunks(n_rows, fn, chunk=ROW_CHUNK):
    def body(c, carry):
        fn(pl.ds(pl.multiple_of(c * chunk, chunk), chunk))
        return carry

    lax.fori_loop(0, n_rows // chunk, body, 0)


def _dot(a, b):
    return jnp.dot(a, b, preferred_element_type=f32)


def _dot_nt(a, b):
    return lax.dot_general(a, b, (((1,), (1,)), ((), ())), preferred_element_type=f32)


def _dot_tn(a, b):
    return lax.dot_general(a, b, (((0,), (0,)), ((), ())), preferred_element_type=f32)


def _dot_onehot_exact(m01, b):
    m = m01.astype(bf16)
    b1 = b.astype(bf16)
    r1 = b - b1.astype(f32)
    b2 = r1.astype(bf16)
    b3 = (r1 - b2.astype(f32)).astype(bf16)
    return _dot(m, b1) + _dot(m, b2) + _dot(m, b3)


def _mod_kernel(c_ref, w_ref, b_ref, o_ref):
    s = _silu(c_ref[...])
    o_ref[0] = jnp.dot(s, w_ref[0], preferred_element_type=f32, precision=lax.Precision.HIGHEST) + b_ref[0]


def _modulation(cond, ada_w, ada_b):
    n = ada_w.shape[-1]
    tn = 1024
    return pl.pallas_call(
        _mod_kernel,
        grid=(DEPTH, n // tn),
        in_specs=[pl.BlockSpec((COND_PAD, D), lambda l, j: (0, 0)),
                  pl.BlockSpec((1, D, tn), lambda l, j: (l, 0, j)),
                  pl.BlockSpec((1, 1, tn), lambda l, j: (l, 0, j))],
        out_specs=pl.BlockSpec((1, COND_PAD, tn), lambda l, j: (l, 0, j)),
        out_shape=jax.ShapeDtypeStruct((DEPTH, COND_PAD, n), f32),
        compiler_params=_cparams(("arbitrary", "arbitrary")),
        name="modulation",
    )(cond, ada_w, ada_b.reshape(DEPTH, 1, n))


def _rope_apply(acc, cos, sin, shift):
    w = acc.shape[1]
    reps = w // cos.shape[1]
    if reps > 1:
        cos = jnp.concatenate([cos] * reps, axis=1)
        sin = jnp.concatenate([sin] * reps, axis=1)
    lane = lax.broadcasted_iota(jnp.int32, acc.shape, 1)
    first = (lane % (2 * shift)) < shift
    rot = jnp.where(first, pltpu.roll(acc, w - shift, axis=1), pltpu.roll(acc, shift, axis=1))
    return acc * cos + rot * sin


def _fused_matmul_kernel(*refs, pre, epi, rope, rope_all, want_pre_out, res_scale, split_at):
    refs = list(refs)
    a_ref = refs.pop(0)
    a_lat_ref = refs.pop(0) if split_at is not None else None
    if pre == "mod":
        nw_ref, sh_ref, sc_ref = refs.pop(0), refs.pop(0), refs.pop(0)
    elif pre == "rms":
        nw_ref = refs.pop(0)
    elif pre == "gated_rms":
        z_ref, nw_ref = refs.pop(0), refs.pop(0)
    w_ref = refs.pop(0)
    if epi == "rope":
        cos_ref, sin_ref = refs.pop(0), refs.pop(0)
    elif epi == "residual":
        res_ref, gate_ref = refs.pop(0), refs.pop(0)
    o_ref = refs.pop(0)
    if want_pre_out:
        po_ref = refs.pop(0)
    h_ref = refs.pop(0)
    j = pl.program_id(1)

    def prologue(rows, src_ref):
        a = src_ref[rows, :].astype(f32)
        if pre == "gated_rms":
            a = a * _silu(z_ref[rows, :])
        if pre != "none":
            a = a * lax.rsqrt(jnp.mean(a * a, axis=-1, keepdims=True) + EPS) * nw_ref[...]
        if pre == "mod":
            a = a * (1.0 + sc_ref[0]) + sh_ref[0]
        if want_pre_out:
            po_ref[rows, :] = a
        h_ref[rows, :] = a.astype(bf16)

    if split_at is None:
        @pl.when(j == 0)
        def _():
            _for_row_chunks(a_ref.shape[0], functools.partial(prologue, src_ref=a_ref))
    else:
        i = pl.program_id(0)

        @pl.when((j == 0) & (i < split_at))
        def _():
            _for_row_chunks(a_ref.shape[0], functools.partial(prologue, src_ref=a_ref))

        @pl.when((j == 0) & (i >= split_at))
        def _():
            _for_row_chunks(a_ref.shape[0], functools.partial(prologue, src_ref=a_lat_ref))

    acc = _dot(h_ref[...], w_ref[...])
    if epi == "rope":
        n_tiles, lo, hi, shift = rope
        if lo == 0 and hi == acc.shape[1]:
            roped = _rope_apply(acc, cos_ref[...], sin_ref[...], shift)
        else:
            mid = _rope_apply(acc[:, lo:hi], cos_ref[...], sin_ref[...], shift)
            roped = jnp.concatenate([acc[:, :lo], mid] + ([acc[:, hi:]] if hi < acc.shape[1] else []), axis=1)
        o_ref[...] = roped if rope_all else jnp.where(j < n_tiles, roped, acc)
    elif epi == "residual":
        o_ref[...] = res_ref[...] + (res_scale * gate_ref[0]) * acc
    else:
        o_ref[...] = acc.astype(o_ref.dtype)


def _fused_matmul(a, w, *, k_blk=0, pre="none", pre_args=(), epi="none", epi_args=(), rope=None,
                  want_pre_out=False, res_scale=1.0, tm=TM, tn=512, name="fused_matmul"):
    k, n = w.shape
    if isinstance(a, tuple):
        rows = a[0].shape[0] + a[1].shape[0]
        n_ctx = a[0].shape[0] // tm
        assert a[0].shape[0] % tm == 0
        ins = list(a)
        specs = [pl.BlockSpec((tm, k), lambda i, j: (jnp.minimum(i, n_ctx - 1), k_blk)),
                 pl.BlockSpec((tm, k), lambda i, j: (jnp.maximum(i - n_ctx, 0), k_blk))]
    else:
        rows = a.shape[0]
        n_ctx = None
        ins, specs = [a], [pl.BlockSpec((tm, k), lambda i, j: (i, k_blk))]
    assert rows % tm == 0 and n % tn == 0
    grid = (rows // tm, n // tn)
    vec_k = pl.BlockSpec((1, k), lambda i, j: (0, 0))
    mod_k = pl.BlockSpec((1, 1, k), lambda i, j: (_mod_row(i, tm), 0, 0))
    if pre == "mod":
        nw, sh, sc = pre_args
        ins += [nw.reshape(1, k), sh, sc]
        specs += [vec_k, mod_k, mod_k]
    elif pre == "rms":
        ins += [pre_args[0].reshape(1, k)]
        specs += [vec_k]
    elif pre == "gated_rms":
        z, nw = pre_args
        ins += [z, nw.reshape(1, k)]
        specs += [pl.BlockSpec((tm, k), lambda i, j: (i, 0)), vec_k]
    ins.append(w)
    if n == tn:
        specs.append(pl.BlockSpec((k, tn), lambda i, j: (0, 0), pipeline_mode=pl.Buffered(1)))
    else:
        specs.append(pl.BlockSpec((k, tn), lambda i, j: (0, j)))
    if epi == "rope":
        cos, sin = epi_args
        p = cos.shape[1]
        ins += [cos, sin]
        specs += [pl.BlockSpec((tm, p), lambda i, j: (i, 0))] * 2
    elif epi == "residual":
        res, gate = epi_args
        ins += [res, gate]
        specs += [pl.BlockSpec((tm, tn), lambda i, j: (i, j)),
                  pl.BlockSpec((1, 1, tn), lambda i, j: (_mod_row(i, tm), 0, j))]
    out_shape = [jax.ShapeDtypeStruct((rows, n), f32)]
    out_specs = [pl.BlockSpec((tm, tn), lambda i, j: (i, j))]
    if want_pre_out:
        out_shape.append(jax.ShapeDtypeStruct((rows, k), f32))
        out_specs.append(pl.BlockSpec((tm, k), lambda i, j: (i, 0)))
    outs = pl.pallas_call(
        functools.partial(_fused_matmul_kernel, pre=pre, epi=epi, rope=rope,
                          rope_all=rope is not None and rope[0] == n // tn, want_pre_out=want_pre_out,
                          res_scale=res_scale, split_at=n_ctx),
        grid=grid, in_specs=specs, out_specs=out_specs, out_shape=out_shape,
        scratch_shapes=[pltpu.VMEM((tm, k), bf16)],
        compiler_params=_cparams(("parallel", "arbitrary")),
        name=name,
    )(*ins)
    return outs if want_pre_out else outs[0]


FFN_TF = 512


def _ffn_kernel(*refs, final_norm, n_ctx, split_in, split_out):
    refs = list(refs)
    x_refs = [refs.pop(0) for _ in range(2 if split_in else 1)]
    nw_ref, sh_ref, sc_ref, gate_ref, wgu_ref, wd_ref, fw_ref = refs[:7]
    refs = refs[7:]
    o_refs = [refs.pop(0) for _ in range(2 if split_out else 1)]
    h_ref, acc_ref = refs
    i = pl.program_id(0)
    f = pl.program_id(1)
    tm = h_ref.shape[0]

    def by_side(n_sides, cond, fn):
        if n_sides == 1:
            pl.when(cond)(functools.partial(fn, 0))
        else:
            pl.when(cond & (i < n_ctx))(functools.partial(fn, 0))
            pl.when(cond & (i >= n_ctx))(functools.partial(fn, 1))

    def prologue(side):
        def rows_fn(rows):
            x = x_refs[side][rows, :]
            h = x * lax.rsqrt(jnp.mean(x * x, axis=-1, keepdims=True) + EPS) * nw_ref[...]
            h_ref[rows, :] = (h * (1.0 + sc_ref[0]) + sh_ref[0]).astype(bf16)

        _for_row_chunks(tm, rows_fn)

    by_side(len(x_refs), f == 0, prologue)

    @pl.when(f == 0)
    def _():
        acc_ref[...] = jnp.zeros_like(acc_ref)

    gu = _dot(h_ref[...], wgu_ref[...])
    tf = gu.shape[1] // 2
    acc_ref[...] += _dot((_silu(gu[:, :tf]) * gu[:, tf:]).astype(bf16), wd_ref[...])

    def epilogue(side_in, side_out):
        def rows_fn(rows):
            y = x_refs[side_in][rows, :] + (0.5 * gate_ref[0]) * acc_ref[rows, :]
            if final_norm:
                y = y * lax.rsqrt(jnp.mean(y * y, axis=-1, keepdims=True) + EPS) * fw_ref[...]
            o_refs[side_out][rows, :] = y

        _for_row_chunks(tm, rows_fn)

    last = f == pl.num_programs(1) - 1
    if split_in or split_out:
        by_side(2, last, lambda side: epilogue(side if split_in else 0, side if split_out else 0))
    else:
        by_side(1, last, lambda side: epilogue(0, 0))


def _ffn_weights(w_gu, w_down):
    nf = D_FF // FFN_TF
    lead = w_gu.shape[:-1]
    wgu = w_gu.reshape(*lead, 2, nf, FFN_TF)
    wgu = jnp.swapaxes(wgu, -3, -2).reshape(*lead, 2 * D_FF)
    return wgu.astype(bf16), w_down.astype(bf16)


def _ffn(x, nw, sh, sc, gate, wgu, wd, layer, half, final_w, *, final_norm, split_out=False, tm=TM):
    nf = D_FF // FFN_TF
    n_ctx = T_P // tm
    split_in = isinstance(x, tuple)
    vec = pl.BlockSpec((1, D), lambda i, f: (0, 0))
    mod = pl.BlockSpec((1, 1, D), lambda i, f: (_mod_row(i, tm), 0, 0))
    ctx_spec = pl.BlockSpec((tm, D), lambda i, f: (jnp.minimum(i, n_ctx - 1), 0))
    lat_spec = pl.BlockSpec((tm, D), lambda i, f: (jnp.maximum(i - n_ctx, 0), 0))
    one_spec = pl.BlockSpec((tm, D), lambda i, f: (i, 0))
    if split_out:
        out_specs = [ctx_spec, lat_spec]
        out_shape = [jax.ShapeDtypeStruct((T_P, D), f32), jax.ShapeDtypeStruct((T_S, D), f32)]
    else:
        out_specs = one_spec
        out_shape = jax.ShapeDtypeStruct((T, D), f32)
    return pl.pallas_call(
        functools.partial(_ffn_kernel, final_norm=final_norm, n_ctx=n_ctx, split_in=split_in, split_out=split_out),
        grid=(T // tm, nf),
        in_specs=([ctx_spec, lat_spec] if split_in else [one_spec]) + [
            vec, mod, mod, mod,
            pl.BlockSpec((None, None, D, 2 * FFN_TF), lambda i, f: (layer, half, 0, f)),
            pl.BlockSpec((None, None, FFN_TF, D), lambda i, f: (layer, half, f, 0)),
            vec],
        out_specs=out_specs,
        out_shape=out_shape,
        scratch_shapes=[pltpu.VMEM((tm, D), bf16), pltpu.VMEM((tm, D), f32)],
        compiler_params=_cparams(("arbitrary" if split_out else "parallel", "arbitrary")),
        name="ffn_final" if split_out else "ffn",
    )(*(x if split_in else (x,)), nw.reshape(1, D), sh, sc, gate, wgu, wd, final_w.reshape(1, D))


LOG2E = math.log2(math.e)


def _exp2_weights(segs):
    m = functools.reduce(jnp.maximum, [jnp.max(s, axis=-1, keepdims=True) for s in segs])
    es = [jnp.exp2(s - m) for s in segs]
    l = functools.reduce(jnp.add, [jnp.sum(e, axis=-1, keepdims=True) for e in es])
    return es, 1.0 / l


def _diff_attn_kernel(*refs, scale, lam_init, has_cache, tq):
    if has_cache:
        lam_ref, sub_ref, q_ref, k_ref, v_ref, kc_ref, vc_ref, o_ref, kb_ref, vb_ref = refs
    else:
        lam_ref, sub_ref, q_ref, k_ref, v_ref, o_ref, kb_ref, vb_ref = refs
    seq = k_ref.shape[0]
    kb_ref[0:seq, :] = k_ref[...].astype(bf16)
    vb_ref[0:seq, :] = v_ref[...].astype(bf16)
    if has_cache:
        kb_ref[seq:seq + PAST, :] = kc_ref[0].astype(bf16)
        vb_ref[seq:seq + PAST, :] = vc_ref[0].astype(bf16)
    lp = lam_ref[...]
    lam = (jnp.exp(jnp.sum(lp[0:1] * lp[1:2], axis=-1, keepdims=True))
           - jnp.exp(jnp.sum(lp[2:3] * lp[3:4], axis=-1, keepdims=True)) + lam_init)
    sub = sub_ref[...]

    def tile(rows):
        q = q_ref[rows, :] * (scale * LOG2E)
        maps = []
        for m in range(2):
            sl = slice(m * DIFF_HD, (m + 1) * DIFF_HD)
            (e,), inv = _exp2_weights([_dot_nt(q[:, sl].astype(bf16), kb_ref[:, sl])])
            maps.append(_dot(e.astype(bf16), vb_ref[...]) * inv)
        o = maps[0] - lam * maps[1]
        o = o * lax.rsqrt(jnp.mean(o * o, axis=-1, keepdims=True) + EPS) * sub
        o_ref[rows, :] = o * (1.0 - lam_init)

    _for_row_chunks(seq, tile, chunk=tq)


def _diff_attention(qkv, lam_p, subln, cache_k, cache_v, *, lam_init, latent, tq=DIFF_TQ):
    hw = 2 * DIFF_HD
    scale = DIFF_HD ** -0.5
    if latent:
        nb, seq, row0 = BS, SS, T_P
    else:
        nb, seq, row0 = BP, SP, 0
    rb = row0 // seq
    nk = seq + (PAST if latent else 0)
    specs = [pl.BlockSpec((4, DIFF_HD), lambda b, h: (0, 0)), pl.BlockSpec((1, hw), lambda b, h: (0, 0)),
             pl.BlockSpec((seq, hw), lambda b, h: (rb + b, h)),
             pl.BlockSpec((seq, hw), lambda b, h: (rb + b, DIFF_HEADS + h)),
             pl.BlockSpec((seq, hw), lambda b, h: (rb + b, 2 * DIFF_HEADS + h))]
    ins = [lam_p, subln.reshape(1, hw), qkv, qkv, qkv]
    if latent:
        specs += [pl.BlockSpec((1, PAST, hw), lambda b, h: (b, 0, h))] * 2
        ins += [cache_k, cache_v]
    return pl.pallas_call(
        functools.partial(_diff_attn_kernel, scale=scale, lam_init=lam_init, has_cache=latent, tq=min(tq, seq)),
        grid=(nb, DIFF_HEADS), in_specs=specs,
        out_specs=pl.BlockSpec((seq, hw), lambda b, h: (b, h)),
        out_shape=jax.ShapeDtypeStruct((nb * seq, D), f32),
        scratch_shapes=[pltpu.VMEM((nk, hw), bf16), pltpu.VMEM((nk, hw), bf16)],
        compiler_params=_cparams(("parallel", "parallel")),
        name="diff_attn_latent" if latent else "diff_attn_context")(*ins)


def _attn_kernel(q_ref, k_ref, v_ref, o_ref, *, scale):
    q = (q_ref[...] * (scale * LOG2E)).astype(bf16)
    (e,), inv = _exp2_weights([_dot_nt(q, k_ref[...].astype(bf16))])
    o_ref[...] = _dot(e.astype(bf16), v_ref[...].astype(bf16)) * inv


def _context_attention(qkv, *, heads, hd, scale):
    return pl.pallas_call(
        functools.partial(_attn_kernel, scale=scale),
        grid=(BP, heads),
        in_specs=[pl.BlockSpec((SP, hd), lambda b, h: (b, h)),
                  pl.BlockSpec((SP, hd), lambda b, h: (b, heads + h)),
                  pl.BlockSpec((SP, hd), lambda b, h: (b, 2 * heads + h))],
        out_specs=pl.BlockSpec((SP, hd), lambda b, h: (b, h)),
        out_shape=jax.ShapeDtypeStruct((T_P, heads * hd), f32),
        compiler_params=_cparams(("parallel", "parallel")),
        name="context_attn")(qkv, qkv, qkv)


def _mla_attn_kernel(*refs, scale, has_cache, tq):
    if has_cache:
        q_ref, kn_ref, kr_ref, v_ref, knc_ref, krc_ref, vc_ref, o_ref, kb_ref, vb_ref = refs
    else:
        q_ref, kn_ref, kr_ref, v_ref, o_ref, kb_ref, vb_ref = refs
    seq = kn_ref.shape[0]
    kb_ref[0:seq, 0:MLA_NOPE] = kn_ref[...].astype(bf16)
    kb_ref[0:seq, MLA_NOPE:] = kr_ref[...].astype(bf16)
    vb_ref[0:seq, :] = v_ref[...].astype(bf16)
    if has_cache:
        kb_ref[seq:seq + PAST, 0:MLA_NOPE] = knc_ref[...].astype(bf16)
        kb_ref[seq:seq + PAST, MLA_NOPE:] = krc_ref[...].astype(bf16)
        vb_ref[seq:seq + PAST, :] = vc_ref[...].astype(bf16)

    def tile(rows):
        q = (q_ref[rows, :] * (scale * LOG2E)).astype(bf16)
        (e,), inv = _exp2_weights([_dot_nt(q, kb_ref[...])])
        o_ref[rows, :] = _dot(e.astype(bf16), vb_ref[...]) * inv

    _for_row_chunks(seq, tile, chunk=tq)


def _mla_attention(q, kvx, lat, kvc, krc, *, latent, tq=MLA_TQ):
    scale = (MLA_NOPE + MLA_ROPE) ** -0.5
    kr_blk = (MLA_Q_LORA + MLA_KV_LORA) // V7X_LANES
    if latent:
        nb, seq, row0 = BS, SS, T_P
    else:
        nb, seq, row0 = BP, SP, 0
    rb = row0 // seq
    nk = seq + (PAST if latent else 0)
    specs = [pl.BlockSpec((seq, MLA_QW), lambda b, h: (rb + b, h)),
             pl.BlockSpec((seq, MLA_NOPE), lambda b, h: (rb + b, 2 * h)),
             pl.BlockSpec((seq, V7X_LANES), lambda b, h: (rb + b, kr_blk)),
             pl.BlockSpec((seq, MLA_V), lambda b, h: (rb + b, 2 * h + 1))]
    ins = [q, kvx, lat, kvx]
    if latent:
        specs += [pl.BlockSpec((PAST, MLA_NOPE), lambda b, h: (b, 2 * h)),
                  pl.BlockSpec((PAST, V7X_LANES), lambda b, h: (b, 0)),
                  pl.BlockSpec((PAST, MLA_V), lambda b, h: (b, 2 * h + 1))]
        ins += [kvc, krc, kvc]
    return pl.pallas_call(
        functools.partial(_mla_attn_kernel, scale=scale, has_cache=latent, tq=min(tq, seq)),
        grid=(nb, MLA_HEADS), in_specs=specs,
        out_specs=pl.BlockSpec((seq, MLA_V), lambda b, h: (b, h)),
        out_shape=jax.ShapeDtypeStruct((nb * seq, MLA_HEADS * MLA_V), f32),
        scratch_shapes=[pltpu.VMEM((nk, MLA_QW), bf16), pltpu.VMEM((nk, MLA_V), bf16)],
        compiler_params=_cparams(("parallel", "parallel")),
        name="mla_attn_latent" if latent else "mla_attn_context")(*ins)


def _na_block_plan():
    plan = []
    for r0 in range(0, GRID_R, NA_QROWS):
        kb = min(max(r0 - NA_WIN_ROWS // 2, 0), GRID_R - NA_KROWS)
        var = 0 if r0 == 0 else (2 if r0 == GRID_R - NA_QROWS else 1)
        plan.append((r0, kb, var))
    return plan


def _na_bias_tables(rpb):
    reps = {var: (r0, kb) for r0, kb, var in _na_block_plan()}
    nr, ncol = 2 * NA_WIN_ROWS - 1, 2 * NA_WIN_COLS - 1
    rsel = np.zeros((3, NA_QROWS, NA_KROWS, nr), np.float32)
    for var in range(3):
        r0, kb = reps[var]
        for i in range(NA_QROWS):
            r = r0 + i
            rs = min(max(r - NA_WIN_ROWS // 2, 0), GRID_R - NA_WIN_ROWS)
            for kk in range(NA_KROWS):
                kr = kb + kk
                if rs <= kr < rs + NA_WIN_ROWS:
                    rsel[var, i, kk, kr - r + NA_WIN_ROWS - 1] = 1.0
    csel = np.zeros((GRID_W, GRID_W, ncol), np.float32)
    for c in range(GRID_W):
        cstart = min(max(c - NA_WIN_COLS // 2, 0), GRID_W - NA_WIN_COLS)
        for kc in range(cstart, cstart + NA_WIN_COLS):
            csel[c, kc, kc - c + NA_WIN_COLS - 1] = 1.0
    ok = (rsel.sum(-1) > 0)[:, None, :, None, :, None] & (csel.sum(-1) > 0)[None, None, None, :, None, :]
    val = jnp.einsum("vikr,hrd->vhikd", rsel, rpb, precision=lax.Precision.HIGHEST)
    tab = jnp.einsum("vhikd,cjd->vhickj", val, csel, precision=lax.Precision.HIGHEST)
    tab = jnp.where(ok, tab * LOG2E, -1e30)
    return tab.reshape(3, NA_HEADS, NA_QROWS * GRID_W, NA_KROWS * GRID_W).astype(f32)


def _na_kernel(q_ref, k_ref, v_ref, ck_ref, cv_ref, bias_ref, o_ref, *, scale):
    ck = ck_ref[0].astype(bf16)
    cv = cv_ref[0].astype(bf16)
    nq = NA_QROWS * GRID_W
    nk = NA_KROWS * GRID_W
    for r0, kb, var in _na_block_plan():
        q = (q_ref[r0 * GRID_W:r0 * GRID_W + nq, :] * (scale * LOG2E)).astype(bf16)
        k = k_ref[kb * GRID_W:kb * GRID_W + nk, :].astype(bf16)
        v = v_ref[kb * GRID_W:kb * GRID_W + nk, :].astype(bf16)
        (e_loc, e_ctx), inv = _exp2_weights([_dot_nt(q, k) + bias_ref[var, 0], _dot_nt(q, ck)])
        o_ref[r0 * GRID_W:r0 * GRID_W + nq, :] = (_dot(e_loc.astype(bf16), v) + _dot(e_ctx.astype(bf16), cv)) * inv


def _na_latent_attention(qkv, cache_k, cache_v, bias):
    scale = NA_HD ** -0.5
    rb = T_P // SS
    nk = NA_KROWS * GRID_W
    return pl.pallas_call(
        functools.partial(_na_kernel, scale=scale),
        grid=(NA_HEADS, BS),
        in_specs=[pl.BlockSpec((SS, NA_HD), lambda h, b: (rb + b, h)),
                  pl.BlockSpec((SS, NA_HD), lambda h, b: (rb + b, NA_HEADS + h)),
                  pl.BlockSpec((SS, NA_HD), lambda h, b: (rb + b, 2 * NA_HEADS + h)),
                  pl.BlockSpec((1, PAST, NA_HD), lambda h, b: (b, 0, h)),
                  pl.BlockSpec((1, PAST, NA_HD), lambda h, b: (b, 0, h)),
                  pl.BlockSpec((3, 1, NA_QROWS * GRID_W, nk), lambda h, b: (0, h, 0, 0))],
        out_specs=pl.BlockSpec((SS, NA_HD), lambda h, b: (b, h)),
        out_shape=jax.ShapeDtypeStruct((T_S, D), f32),
        compiler_params=_cparams(("parallel", "arbitrary")),
        name="na_attn_latent",
    )(qkv, qkv, qkv, cache_k, cache_v, bias)


SSD_PAIRS = SSD_HPG // 2
SSD_DLANES = 16


def _ssd_kernel(*refs, seq, has_h0, want_state):
    refs = list(refs)
    (x_ref, b_ref, c_ref, dt_ref, cwx_ref, cwb_ref, cwc_ref, cbx_ref, cbb_ref, cbc_ref,
     alog_ref, dtb_ref, dsk_ref) = refs[:13]
    refs = refs[13:]
    h0_ref = refs.pop(0) if has_h0 else None
    y_ref = refs.pop(0)
    hT_ref = refs.pop(0) if want_state else None
    xp_ref, bp_ref, cp_ref, xs_ref, bt_ref, cs_ref, ac_ref, act_ref, dtt_ref, h_ref = refs
    nc = seq // SSD_Q
    halo = 8

    for src, pad, dst, cw_ref, cb_ref, transpose in ((x_ref, xp_ref, xs_ref, cwx_ref, cbx_ref, False),
                                                     (b_ref, bp_ref, bt_ref, cwb_ref, cbb_ref, True),
                                                     (c_ref, cp_ref, cs_ref, cwc_ref, cbc_ref, False)):
        width = src.shape[1]
        pad[0:halo, :] = jnp.zeros((halo, width), f32)
        pad[halo + seq:2 * halo + seq, :] = jnp.zeros((halo, width), f32)
        pad[halo:halo + seq, :] = src[...]
        cw = cw_ref[...]
        cb = cb_ref[...]

        def conv_chunk(c, carry, pad=pad, dst=dst, cw=cw, cb=cb, width=width, transpose=transpose):
            base = pl.multiple_of(c * SSD_Q, SSD_Q)
            win = pad[pl.ds(base, SSD_Q + 2 * halo), :]
            acc = jnp.broadcast_to(cb, (SSD_Q, width))
            for t in range(SSD_CONV_W):
                off = halo - SSD_CONV_W // 2 + t
                sh = pltpu.roll(win, SSD_Q + 2 * halo - off, axis=0)[:SSD_Q]
                acc = acc + sh * cw[t:t + 1, :]
            out = _silu(acc)
            dst[pl.ds(base, SSD_Q), :] = out.T if transpose else out
            return carry

        lax.fori_loop(0, nc, conv_chunk, 0)

    row = lax.broadcasted_iota(jnp.int32, (SSD_Q, SSD_Q), 0)
    col = lax.broadcasted_iota(jnp.int32, (SSD_Q, SSD_Q), 1)
    tri = {0: (col <= row), 1: (col >= row)}
    low_half = col < SSD_P
    a_neg = -jnp.exp(alog_ref[0])
    dtb = dtb_ref[0]
    dsk = dsk_ref[...]
    tril = tri[0].astype(f32)

    def dt_chunk(c, carry):
        base = pl.multiple_of(c * SSD_Q, SSD_Q)
        dt = dt_ref[pl.ds(base, SSD_Q), :] + dtb
        dt = jnp.maximum(dt, 0.0) + jnp.log1p(jnp.exp(-jnp.abs(dt)))
        a = dt * a_neg
        pre = _dot_onehot_exact(tril, a)
        suf = pre[SSD_Q - 1:SSD_Q, :] - pre + a
        acum = jnp.where(col < SSD_HPG, pre, suf)
        ac_ref[pl.ds(base, SSD_Q), :] = acum
        tb = pl.multiple_of(c * SSD_DLANES, SSD_DLANES)
        act_ref[pl.ds(tb, SSD_DLANES), :] = acum.T[:SSD_DLANES]
        dtt_ref[pl.ds(tb, SSD_DLANES), :] = dt.T[:SSD_DLANES]
        return carry

    lax.fori_loop(0, nc, dt_chunk, 0)

    for direction in range(2):
        mask = tri[direction]
        o = direction * SSD_HPG
        for k in range(SSD_PAIRS):
            if has_h0:
                h_ref[k] = h0_ref[0, direction, k * SSD_Q:(k + 1) * SSD_Q, :].T
            else:
                h_ref[k] = jnp.zeros((SSD_N, 2 * SSD_P), f32)

        def chunk(ci, carry, direction=direction, mask=mask, o=o):
            c = ci if direction == 0 else nc - 1 - ci
            base = pl.multiple_of(c * SSD_Q, SSD_Q)
            rows = pl.ds(base, SSD_Q)
            tb = pl.multiple_of(c * SSD_DLANES, SSD_DLANES)
            xc = xs_ref[rows, :]
            cc = cs_ref[rows, :]
            bt = bt_ref[rows, :]
            acum = ac_ref[rows, :]
            acum_t = act_ref[pl.ds(tb, SSD_DLANES), :]
            dt_t = dtt_ref[pl.ds(tb, SSD_DLANES), :]
            tot = acum_t[:, SSD_Q - 1:SSD_Q] if direction == 0 else acum_t[:, 0:1]
            w_t = dt_t * jnp.exp(tot - acum_t)
            decay = jnp.exp(tot)
            cb = _dot(cc.astype(bf16), bt.astype(bf16))
            ys = []
            for k in range(SSD_PAIRS):
                xk = xc[:, k * 2 * SSD_P:(k + 1) * 2 * SSD_P].astype(bf16)
                hk = h_ref[k]
                y_pair = None
                st_pair = None
                for e in range(2):
                    j = o + 2 * k + e
                    half = low_half if e == 0 else jnp.logical_not(low_half)
                    a_q = jnp.broadcast_to(acum[:, j:j + 1], (SSD_Q, SSD_Q))
                    seg = a_q - acum_t[j:j + 1, :]
                    lmat = jnp.where(mask, jnp.exp(jnp.where(mask, seg, 0.0)), 0.0)
                    lhs = jnp.concatenate([cb * lmat * dt_t[j:j + 1, :], cc * jnp.exp(a_q)], axis=1).astype(bf16)
                    x_e = jnp.where(half, xk, jnp.zeros_like(xk))
                    h_e = jnp.where(half, hk, 0.0).astype(bf16)
                    part = _dot(lhs, jnp.concatenate([x_e, h_e], axis=0))
                    st = _dot((bt * w_t[j:j + 1, :]).astype(bf16), x_e)
                    y_pair = part if y_pair is None else y_pair + part
                    st_pair = st if st_pair is None else st_pair + st
                d_row = jnp.where(low_half[0:1], decay[o + 2 * k:o + 2 * k + 1, :], decay[o + 2 * k + 1:o + 2 * k + 2, :])
                h_ref[k] = hk * d_row + st_pair
                ys.append(y_pair)
            y = jnp.concatenate(ys, axis=1)
            if direction == 0:
                y_ref[rows, :] = y + xc * dsk
            else:
                y_ref[rows, :] += y
            return carry

        lax.fori_loop(0, nc, chunk, 0)
        if want_state:
            for k in range(SSD_PAIRS):
                hT_ref[0, direction, k * SSD_Q:(k + 1) * SSD_Q, :] = h_ref[k].T


def _ssd_group_dt_columns(m):
    rows = m.shape[0]
    fb = m.reshape(rows, 2, SSD_G, SSD_HPG).transpose(0, 2, 1, 3).reshape(rows, SSD_G, SSD_DLANES)
    return jnp.pad(fb, ((0, 0), (0, 0), (0, V7X_LANES - SSD_DLANES))).reshape(rows, SSD_G * V7X_LANES)


def _ssd_scan(proj, conv_w, conv_b, a_log, dt_bias, d_skip, h0, *, latent):
    if latent:
        nb, seq, row0 = BS, SS, T_P
    else:
        nb, seq, row0 = BP, SP, 0
    rb = row0 // seq
    xb0 = SSD_DI // SSD_GW
    bb0 = 2 * SSD_DI // SSD_N
    cb0 = bb0 + SSD_G
    dtb0 = (SSD_DI + SSD_CONV_CH) // V7X_LANES
    cwb0 = SSD_DI // SSD_N
    row = lambda blk: (lambda b, g: (rb + b, blk(g)))
    grp = pl.BlockSpec((1, 1, V7X_LANES), lambda b, g: (g, 0, 0))
    specs = [pl.BlockSpec((seq, SSD_GW), row(lambda g: xb0 + g)),
             pl.BlockSpec((seq, SSD_N), row(lambda g: bb0 + g)),
             pl.BlockSpec((seq, SSD_N), row(lambda g: cb0 + g)),
             pl.BlockSpec((seq, V7X_LANES), row(lambda g: dtb0 + g)),
             pl.BlockSpec((SSD_CONV_W, SSD_GW), lambda b, g: (0, g)),
             pl.BlockSpec((SSD_CONV_W, SSD_N), lambda b, g: (0, cwb0 + g)),
             pl.BlockSpec((SSD_CONV_W, SSD_N), lambda b, g: (0, cwb0 + SSD_G + g)),
             pl.BlockSpec((1, SSD_GW), lambda b, g: (0, g)),
             pl.BlockSpec((1, SSD_N), lambda b, g: (0, cwb0 + g)),
             pl.BlockSpec((1, SSD_N), lambda b, g: (0, cwb0 + SSD_G + g)),
             grp, grp,
             pl.BlockSpec((1, SSD_GW), lambda b, g: (0, g))]
    cbr = conv_b.reshape(1, SSD_CONV_CH)
    ins = [proj, proj, proj, proj, conv_w, conv_w, conv_w, cbr, cbr, cbr,
           _ssd_group_dt_columns(a_log.reshape(1, 2 * SSD_HEADS)).reshape(SSD_G, 1, V7X_LANES),
           _ssd_group_dt_columns(dt_bias.reshape(1, 2 * SSD_HEADS)).reshape(SSD_G, 1, V7X_LANES),
           jnp.repeat(d_skip, SSD_P).reshape(1, SSD_DI)]
    state_spec = pl.BlockSpec((1, 2, SSD_GW, SSD_N), lambda b, g: (b, 0, g, 0))
    if latent:
        specs.append(state_spec)
        ins.append(h0)
    out_shape = [jax.ShapeDtypeStruct((nb * seq, SSD_DI), f32)]
    out_specs = [pl.BlockSpec((seq, SSD_GW), lambda b, g: (b, g))]
    if not latent:
        out_shape.append(jax.ShapeDtypeStruct((nb, 2, SSD_HEADS * SSD_P, SSD_N), f32))
        out_specs.append(state_spec)
    pad_rows = seq + 16
    return pl.pallas_call(
        functools.partial(_ssd_kernel, seq=seq, has_h0=latent, want_state=not latent),
        grid=(nb, SSD_G), in_specs=specs, out_specs=out_specs, out_shape=out_shape,
        scratch_shapes=[pltpu.VMEM((pad_rows, SSD_GW), f32), pltpu.VMEM((pad_rows, SSD_N), f32),
                        pltpu.VMEM((pad_rows, SSD_N), f32),
                        pltpu.VMEM((seq, SSD_GW), f32), pltpu.VMEM((seq, SSD_Q), f32), pltpu.VMEM((seq, SSD_N), f32),
                        pltpu.VMEM((seq, V7X_LANES), f32),
                        pltpu.VMEM((seq // SSD_Q * SSD_DLANES, SSD_Q), f32),
                        pltpu.VMEM((seq // SSD_Q * SSD_DLANES, SSD_Q), f32),
                        pltpu.VMEM((SSD_PAIRS, SSD_N, 2 * SSD_P), f32)],
        compiler_params=_cparams(("parallel", "arbitrary")),
        name="ssd_latent" if latent else "ssd_context")(*ins)


def _axial_tables(half):
    t = np.arange(SS)
    n = half // 2
    inv = ROPE_BASE ** (-np.arange(n, dtype=np.float32) / n)
    cos, sin = [], []
    for pos in (t // GRID_W, t % GRID_W):
        ang = pos.astype(np.float32)[:, None] * inv[None].astype(np.float32)
        cos += [np.cos(ang), np.cos(ang)]
        sin += [-np.sin(ang), np.sin(ang)]
    return np.concatenate(cos, axis=1).astype(np.float32), np.concatenate(sin, axis=1).astype(np.float32)


def _token_tables(cos_lat, sin_lat):
    w = cos_lat.shape[1]
    cos = np.concatenate([np.ones((T_P, w), np.float32), np.tile(cos_lat, (BS, 1))], axis=0)
    sin = np.concatenate([np.zeros((T_P, w), np.float32), np.tile(sin_lat, (BS, 1))], axis=0)
    return jnp.asarray(cos), jnp.asarray(sin)


def _rope_tables():
    c128, s128 = _axial_tables(DIFF_HD // 2)
    c64, s64 = _axial_tables(MLA_ROPE // 2)
    one, zero = np.ones((SS, 64), np.float32), np.zeros((SS, 64), np.float32)
    one2, zero2 = np.ones((SS, 128), np.float32), np.zeros((SS, 128), np.float32)
    diff = _token_tables(c128, s128)
    mla_q = _token_tables(np.concatenate([one2, c64, one], 1), np.concatenate([zero2, s64, zero], 1))
    mla_kr = _token_tables(np.concatenate([c64, one], 1), np.concatenate([s64, zero], 1))
    return diff, mla_q, mla_kr


def kernel(x_prompt, x_sample, cache_diff_k, cache_diff_v, cache_na_k, cache_na_v, cache_mla_ckv, cache_mla_krope,
           state_ssd, c, c_ctx, ada_w, ada_b, norm_w, ffn_w_gu, ffn_w_down, final_norm_w, diff_w_qkv, diff_w_o,
           diff_lambda, diff_subln_w, na_w_qkv, na_w_o, na_rpb, mla_w_dq, mla_q_norm, mla_w_uq, mla_w_dkv,
           mla_kv_norm, mla_w_ukv, mla_w_o, ssd_w_in, ssd_conv_w, ssd_conv_b, ssd_a_log, ssd_dt_bias, ssd_d,
           ssd_norm_w, ssd_w_out):
    x = (x_prompt.reshape(T_P, D), x_sample.reshape(T_S, D))
    wgu, wd = _ffn_weights(ffn_w_gu, ffn_w_down)
    cond =jnp.concatenate([c_ctx[None], c, jnp.zeros((COND_PAD - N_COND, D), f32)], axis=0)
    mods = _modulation(cond, ada_w, ada_b)[:, :N_COND].reshape(DEPTH, N_COND, 3, 3, 1, D)
    rope_diff, rope_mla_q, rope_mla_kr = _rope_tables()

    def mod(i, k, what):
        return mods[i, :, k, what]

    def o_proj(a, w_o, i, **kw):
        return _fused_matmul(a, w_o.astype(bf16), epi="residual", epi_args=(x, mod(i, 1, 2)), tn=D, **kw)

    outs = {}
    for i in range(DEPTH):
        x = _ffn(x, norm_w[i, 0], mod(i, 0, 0), mod(i, 0, 1), mod(i, 0, 2), wgu, wd, i, 0, final_norm_w,
                 final_norm=False)
        pre_args = (norm_w[i, 1], mod(i, 1, 0), mod(i, 1, 1))
        kind = i % 4
        if kind == 0:
            lam_init = 0.8 - 0.6 * math.exp(-0.3 * i)
            qkv = _fused_matmul(x, diff_w_qkv[0].astype(bf16), pre="mod", pre_args=pre_args, epi="rope",
                                epi_args=rope_diff, rope=(2 * D // QKV_TN, 0, QKV_TN, DIFF_HD // 4),
                                tm=QKV_TM, tn=QKV_TN, name="diff_qkv")
            outs["diff_k"] = qkv[:T_P, D:2 * D].reshape(BP, 1, SP, DIFF_HEADS, 2, DIFF_HD)
            outs["diff_v"] = qkv[:T_P, 2 * D:].reshape(BP, 1, SP, DIFF_HEADS, 2 * DIFF_HD)
            o_ctx = _diff_attention(qkv, diff_lambda[0], diff_subln_w[0], None, None, lam_init=lam_init, latent=False)
            o_lat = _diff_attention(qkv, diff_lambda[0], diff_subln_w[0], cache_diff_k.reshape(BS, PAST, D),
                                    cache_diff_v.reshape(BS, PAST, D), lam_init=lam_init, latent=True)
            x = o_proj((o_ctx, o_lat), diff_w_o[0], i, name="diff_out")
        elif kind == 1:
            qkv = _fused_matmul(x, na_w_qkv[0].astype(bf16), pre="mod", pre_args=pre_args, tm=QKV_TM, tn=QKV_TN,
                                name="na_qkv")
            outs["na_k"] = qkv[:T_P, D:2 * D].reshape(BP, 1, SP, NA_HEADS, NA_HD)
            outs["na_v"] = qkv[:T_P, 2 * D:].reshape(BP, 1, SP, NA_HEADS, NA_HD)
            o_ctx = _context_attention(qkv, heads=NA_HEADS, hd=NA_HD, scale=NA_HD ** -0.5)
            o_lat = _na_latent_attention(qkv, cache_na_k.reshape(BS, PAST, D), cache_na_v.reshape(BS, PAST, D),
                                         _na_bias_tables(na_rpb[0]))
            x = o_proj((o_ctx, o_lat), na_w_o[0], i, name="na_out")
        elif kind == 2:
            w_lat = jnp.concatenate([mla_w_dq[0], mla_w_dkv[0],
                                     jnp.zeros((D, V7X_LANES - MLA_ROPE), f32)], axis=1).astype(bf16)
            n_lat = w_lat.shape[1]
            lat = _fused_matmul(x, w_lat, pre="mod", pre_args=pre_args, epi="rope", epi_args=rope_mla_kr,
                                rope=(1, n_lat - V7X_LANES, n_lat, MLA_ROPE // 4), tn=n_lat, name="mla_down")
            w_uq = jnp.pad(mla_w_uq[0].reshape(MLA_Q_LORA, MLA_HEADS, MLA_NOPE + MLA_ROPE),
                           ((0, 0), (0, 0), (0, MLA_QW - MLA_NOPE - MLA_ROPE))).reshape(MLA_Q_LORA, MLA_HEADS * MLA_QW)
            q = _fused_matmul(lat, w_uq.astype(bf16), pre="rms", pre_args=(mla_q_norm[0],), epi="rope",
                              epi_args=rope_mla_q, rope=(MLA_HEADS * MLA_QW // QKV_TN, 0, QKV_TN, MLA_ROPE // 4),
                              tm=QKV_TM, tn=QKV_TN, name="mla_q")
            w_ukv = mla_w_ukv[0].astype(bf16)
            kvx, ckv_n = _fused_matmul(lat, w_ukv, k_blk=MLA_Q_LORA // MLA_KV_LORA, pre="rms",
                                       pre_args=(mla_kv_norm[0],), want_pre_out=True, tm=QKV_TM, tn=QKV_TN,
                                       name="mla_kv")
            kvc = _fused_matmul(cache_mla_ckv.reshape(BS * PAST, MLA_KV_LORA), w_ukv, tm=QKV_TM, tn=QKV_TN,
                                name="mla_kv_cache")
            krc = jnp.pad(cache_mla_krope.reshape(BS * PAST, MLA_ROPE), ((0, 0), (0, V7X_LANES - MLA_ROPE)))
            outs["mla_ckv"] = ckv_n[:T_P].reshape(BP, 1, SP, MLA_KV_LORA)
            outs["mla_kr"] = lat[:T_P, MLA_Q_LORA + MLA_KV_LORA:MLA_Q_LORA + MLA_KV_LORA + MLA_ROPE].reshape(
                BP, 1, SP, MLA_ROPE)
            o_ctx = _mla_attention(q, kvx, lat, None, None, latent=False)
            o_lat = _mla_attention(q, kvx, lat, kvc, krc, latent=True)
            x = o_proj((o_ctx, o_lat), mla_w_o[0], i, name="mla_out")
        else:
            n_xz = SSD_DI + SSD_CONV_CH
            w_in = jnp.concatenate([ssd_w_in[0][:, :n_xz], _ssd_group_dt_columns(ssd_w_in[0][:, n_xz:])], axis=1)
            proj = _fused_matmul(x, w_in.astype(bf16), pre="mod", pre_args=pre_args, tm=QKV_TM, tn=QKV_TN,
                                 name="ssd_in")
            args = (proj, ssd_conv_w[0], ssd_conv_b[0], ssd_a_log[0], ssd_dt_bias[0], ssd_d[0])
            y_ctx, h_t = _ssd_scan(*args, None, latent=False)
            (y_lat,) = _ssd_scan(*args, state_ssd.reshape(BS, 2, SSD_HEADS * SSD_P, SSD_N), latent=True)
            outs["ssd_h"] = h_t.reshape(BP, 1, 2, SSD_HEADS, SSD_P, SSD_N)
            x = o_proj((y_ctx, y_lat), ssd_w_out[0], i, pre="gated_rms", pre_args=(proj, ssd_norm_w[0]), tm=TM // 2,
                       name="ssd_out")
        last = i == DEPTH - 1
        x = _ffn(x, norm_w[i, 2], mod(i, 2, 0), mod(i, 2, 1), mod(i, 2, 2), wgu, wd, i, 1, final_norm_w,
                 final_norm=last, split_out=last)
    y_ctx, y_lat = x
    return (y_ctx.reshape(BP, SP, D), y_lat.reshape(BS, SS, D), outs["diff_k"], outs["diff_v"], outs["na_k"],
            outs["na_v"], outs["mla_ckv"], outs["mla_kr"], outs["ssd_h"])
```

```python
import functools
import math

import numpy as np
import jax
import jax.numpy as jnp
from jax import lax
from jax.experimental import pallas as pl
from jax.experimental.pallas import tpu as pltpu

f32 = jnp.float32
bf16 = jnp.bfloat16

D = 2048
BP, SP = 16, 256
BS, SS = 8, 2048
PAST = 512
GRID_W = 64
GRID_R = SS // GRID_W
DEPTH = 4
D_FF = 5632
EPS = 1e-6
ROPE_BASE = 10000.0
T_P = BP * SP
T_S = BS * SS
T = T_P + T_S
N_COND = 1 + BS
COND_PAD = 16

DIFF_HEADS, DIFF_HD = 8, 128
NA_HEADS, NA_HD = 16, 128
NA_WIN_ROWS, NA_WIN_COLS = 8, 16
NA_QROWS = 4
NA_KROWS = 12
MLA_HEADS, MLA_Q_LORA, MLA_KV_LORA = 16, 512, 256
MLA_NOPE, MLA_ROPE, MLA_V = 128, 64, 128
MLA_QW = 256
SSD_DI = 2 * D
SSD_P, SSD_HEADS, SSD_G, SSD_HPG, SSD_N = 64, 64, 8, 8, 128
SSD_CONV_W, SSD_Q = 7, 128
SSD_CONV_CH = SSD_DI + 2 * SSD_G * SSD_N
SSD_IN = SSD_DI + SSD_CONV_CH + 2 * SSD_HEADS
SSD_GW = SSD_HPG * SSD_P

V7X_LANES = 128
V7X_VMEM_LIMIT = 60000 * 1024

TM = 512
QKV_TM, QKV_TN = 1024, 1024
DIFF_TQ, MLA_TQ = 512, 256
ROW_CHUNK = 256


def _cparams(sem):
    return pltpu.CompilerParams(dimension_semantics=sem, vmem_limit_bytes=V7X_VMEM_LIMIT)


def _mod_row(i, tm):
    start = i * tm
    return jnp.where(start < T_P, 0, 1 + (start - T_P) // SS)


def _silu(x):
    return x / (1.0 + jnp.exp(-x))


def _for_row_chunks(n_rows, fn, chunk=ROW_CHUNK):
    def body(c, carry):
        fn(pl.ds(pl.multiple_of(c * chunk, chunk), chunk))
        return carry

    lax.fori_loop(0, n_rows // chunk, body, 0)


def _dot(a, b):
    return jnp.dot(a, b, preferred_element_type=f32)


def _dot_nt(a, b):
    return lax.dot_general(a, b, (((1,), (1,)), ((), ())), preferred_element_type=f32)


def _dot_tn(a, b):
    return lax.dot_general(a, b, (((0,), (0,)), ((), ())), preferred_element_type=f32)


def _dot_onehot_exact(m01, b):
    m = m01.astype(bf16)
    b1 = b.astype(bf16)
    r1 = b - b1.astype(f32)
    b2 = r1.astype(bf16)
    b3 = (r1 - b2.astype(f32)).astype(bf16)
    return _dot(m, b1) + _dot(m, b2) + _dot(m, b3)


def _mod_kernel(c_ref, w_ref, b_ref, o_ref):
    s = _silu(c_ref[...])
    o_ref[0] = jnp.dot(s, w_ref[0], preferred_element_type=f32, precision=lax.Precision.HIGHEST) + b_ref[0]


def _modulation(cond, ada_w, ada_b):
    n = ada_w.shape[-1]
    tn = 1024
    return pl.pallas_call(
        _mod_kernel,
        grid=(DEPTH, n // tn),
        in_specs=[pl.BlockSpec((COND_PAD, D), lambda l, j: (0, 0)),
                  pl.BlockSpec((1, D, tn), lambda l, j: (l, 0, j)),
                  pl.BlockSpec((1, 1, tn), lambda l, j: (l, 0, j))],
        out_specs=pl.BlockSpec((1, COND_PAD, tn), lambda l, j: (l, 0, j)),
        out_shape=jax.ShapeDtypeStruct((DEPTH, COND_PAD, n), f32),
        compiler_params=_cparams(("arbitrary", "arbitrary")),
        name="modulation",
    )(cond, ada_w, ada_b.reshape(DEPTH, 1, n))


def _rope_apply(acc, cos, sin, shift):
    w = acc.shape[1]
    reps = w // cos.shape[1]
    if reps > 1:
        cos = jnp.concatenate([cos] * reps, axis=1)
        sin = jnp.concatenate([sin] * reps, axis=1)
    lane = lax.broadcasted_iota(jnp.int32, acc.shape, 1)
    first = (lane % (2 * shift)) < shift
    rot = jnp.where(first, pltpu.roll(acc, w - shift, axis=1), pltpu.roll(acc, shift, axis=1))
    return acc * cos + rot * sin


def _fused_matmul_kernel(*refs, pre, epi, rope, rope_all, want_pre_out, res_scale, split_at):
    refs = list(refs)
    a_ref = refs.pop(0)
    a_lat_ref = refs.pop(0) if split_at is not None else None
    if pre == "mod":
        nw_ref, sh_ref, sc_ref = refs.pop(0), refs.pop(0), refs.pop(0)
    elif pre == "rms":
        nw_ref = refs.pop(0)
    elif pre == "gated_rms":
        z_ref, nw_ref = refs.pop(0), refs.pop(0)
    w_ref = refs.pop(0)
    if epi == "rope":
        cos_ref, sin_ref = refs.pop(0), refs.pop(0)
    elif epi == "residual":
        res_ref, gate_ref = refs.pop(0), refs.pop(0)
    o_ref = refs.pop(0)
    if want_pre_out:
        po_ref = refs.pop(0)
    h_ref = refs.pop(0)
    j = pl.program_id(1)

    def prologue(rows, src_ref):
        a = src_ref[rows, :].astype(f32)
        if pre == "gated_rms":
            a = a * _silu(z_ref[rows, :])
        if pre != "none":
            a = a * lax.rsqrt(jnp.mean(a * a, axis=-1, keepdims=True) + EPS) * nw_ref[...]
        if pre == "mod":
            a = a * (1.0 + sc_ref[0]) + sh_ref[0]
        if want_pre_out:
            po_ref[rows, :] = a
        h_ref[rows, :] = a.astype(bf16)

    if split_at is None:
        @pl.when(j == 0)
        def _():
            _for_row_chunks(a_ref.shape[0], functools.partial(prologue, src_ref=a_ref))
    else:
        i = pl.program_id(0)

        @pl.when((j == 0) & (i < split_at))
        def _():
            _for_row_chunks(a_ref.shape[0], functools.partial(prologue, src_ref=a_ref))

        @pl.when((j == 0) & (i >= split_at))
        def _():
            _for_row_chunks(a_ref.shape[0], functools.partial(prologue, src_ref=a_lat_ref))

    acc = _dot(h_ref[...], w_ref[...])
    if epi == "rope":
        n_tiles, lo, hi, shift = rope
        if lo == 0 and hi == acc.shape[1]:
            roped = _rope_apply(acc, cos_ref[...], sin_ref[...], shift)
        else:
            mid = _rope_apply(acc[:, lo:hi], cos_ref[...], sin_ref[...], shift)
            roped = jnp.concatenate([acc[:, :lo], mid] + ([acc[:, hi:]] if hi < acc.shape[1] else []), axis=1)
        o_ref[...] = roped if rope_all else jnp.where(j < n_tiles, roped, acc)
    elif epi == "residual":
        o_ref[...] = res_ref[...] + (res_scale * gate_ref[0]) * acc
    else:
        o_ref[...] = acc.astype(o_ref.dtype)


def _fused_matmul(a, w, *, k_blk=0, pre="none", pre_args=(), epi="none", epi_args=(), rope=None,
                  want_pre_out=False, res_scale=1.0, tm=TM, tn=512, name="fused_matmul"):
    k, n = w.shape
    if isinstance(a, tuple):
        rows = a[0].shape[0] + a[1].shape[0]
        n_ctx = a[0].shape[0] // tm
        assert a[0].shape[0] % tm == 0
        ins = list(a)
        specs = [pl.BlockSpec((tm, k), lambda i, j: (jnp.minimum(i, n_ctx - 1), k_blk)),
                 pl.BlockSpec((tm, k), lambda i, j: (jnp.maximum(i - n_ctx, 0), k_blk))]
    else:
        rows = a.shape[0]
        n_ctx = None
        ins, specs = [a], [pl.BlockSpec((tm, k), lambda i, j: (i, k_blk))]
    assert rows % tm == 0 and n % tn == 0
    grid = (rows // tm, n // tn)
    vec_k = pl.BlockSpec((1, k), lambda i, j: (0, 0))
    mod_k = pl.BlockSpec((1, 1, k), lambda i, j: (_mod_row(i, tm), 0, 0))
    if pre == "mod":
        nw, sh, sc = pre_args
        ins += [nw.reshape(1, k), sh, sc]
        specs += [vec_k, mod_k, mod_k]
    elif pre == "rms":
        ins += [pre_args[0].reshape(1, k)]
        specs += [vec_k]
    elif pre == "gated_rms":
        z, nw = pre_args
        ins += [z, nw.reshape(1, k)]
        specs += [pl.BlockSpec((tm, k), lambda i, j: (i, 0)), vec_k]
    ins.append(w)
    if n == tn:
        specs.append(pl.BlockSpec((k, tn), lambda i, j: (0, 0), pipeline_mode=pl.Buffered(1)))
    else:
        specs.append(pl.BlockSpec((k, tn), lambda i, j: (0, j)))
    if epi == "rope":
        cos, sin = epi_args
        p = cos.shape[1]
        ins += [cos, sin]
        specs += [pl.BlockSpec((tm, p), lambda i, j: (i, 0))] * 2
    elif epi == "residual":
        res, gate = epi_args
        ins += [res, gate]
        specs += [pl.BlockSpec((tm, tn), lambda i, j: (i, j)),
                  pl.BlockSpec((1, 1, tn), lambda i, j: (_mod_row(i, tm), 0, j))]
    out_shape = [jax.ShapeDtypeStruct((rows, n), f32)]
    out_specs = [pl.BlockSpec((tm, tn), lambda i, j: (i, j))]
    if want_pre_out:
        out_shape.append(jax.ShapeDtypeStruct((rows, k), f32))
        out_specs.append(pl.BlockSpec((tm, k), lambda i, j: (i, 0)))
    outs = pl.pallas_call(
        functools.partial(_fused_matmul_kernel, pre=pre, epi=epi, rope=rope,
                          rope_all=rope is not None and rope[0] == n // tn, want_pre_out=want_pre_out,
                          res_scale=res_scale, split_at=n_ctx),
        grid=grid, in_specs=specs, out_specs=out_specs, out_shape=out_shape,
        scratch_shapes=[pltpu.VMEM((tm, k), bf16)],
        compiler_params=_cparams(("parallel", "arbitrary")),
        name=name,
    )(*ins)
    return outs if want_pre_out else outs[0]


FFN_TF = 512


def _ffn_kernel(*refs, final_norm, n_ctx, split_in, split_out):
    refs = list(refs)
    x_refs = [refs.pop(0) for _ in range(2 if split_in else 1)]
    nw_ref, sh_ref, sc_ref, gate_ref, wg_ref, wu_ref, wd_ref, fw_ref = refs[:8]
    refs = refs[8:]
    o_refs = [refs.pop(0) for _ in range(2 if split_out else 1)]
    h_ref, acc_ref = refs
    i = pl.program_id(0)
    f = pl.program_id(1)
    tm = h_ref.shape[0]

    def by_side(n_sides, cond, fn):
        if n_sides == 1:
            pl.when(cond)(functools.partial(fn, 0))
        else:
            pl.when(cond & (i < n_ctx))(functools.partial(fn, 0))
            pl.when(cond & (i >= n_ctx))(functools.partial(fn, 1))

    def prologue(side):
        def rows_fn(rows):
            x = x_refs[side][rows, :]
            h = x * lax.rsqrt(jnp.mean(x * x, axis=-1, keepdims=True) + EPS) * nw_ref[...]
            h_ref[rows, :] = (h * (1.0 + sc_ref[0]) + sh_ref[0]).astype(bf16)

        _for_row_chunks(tm, rows_fn)

    by_side(len(x_refs), f == 0, prologue)

    @pl.when(f == 0)
    def _():
        acc_ref[...] = jnp.zeros_like(acc_ref)

    h = h_ref[...]
    g = _dot(h, wg_ref[...])
    u = _dot(h, wu_ref[...])
    acc_ref[...] += _dot((_silu(g) * u).astype(bf16), wd_ref[...])

    def epilogue(side_in, side_out):
        def rows_fn(rows):
            y = x_refs[side_in][rows, :] + (0.5 * gate_ref[0]) * acc_ref[rows, :]
            if final_norm:
                y = y * lax.rsqrt(jnp.mean(y * y, axis=-1, keepdims=True) + EPS) * fw_ref[...]
            o_refs[side_out][rows, :] = y

        _for_row_chunks(tm, rows_fn)

    last = f == pl.num_programs(1) - 1
    if split_in or split_out:
        by_side(2, last, lambda side: epilogue(side if split_in else 0, side if split_out else 0))
    else:
        by_side(1, last, lambda side: epilogue(0, 0))


def _ffn_weights(w_gu, w_down):
    return w_gu.astype(bf16), w_down.astype(bf16)


def _ffn(x, nw, sh, sc, gate, wgu, wd, layer, half, final_w, *, final_norm, split_out=False, tm=TM):
    nf = D_FF // FFN_TF
    n_ctx = T_P // tm
    split_in = isinstance(x, tuple)
    vec = pl.BlockSpec((1, D), lambda i, f: (0, 0))
    mod = pl.BlockSpec((1, 1, D), lambda i, f: (_mod_row(i, tm), 0, 0))
    ctx_spec = pl.BlockSpec((tm, D), lambda i, f: (jnp.minimum(i, n_ctx - 1), 0))
    lat_spec = pl.BlockSpec((tm, D), lambda i, f: (jnp.maximum(i - n_ctx, 0), 0))
    one_spec = pl.BlockSpec((tm, D), lambda i, f: (i, 0))
    if split_out:
        out_specs = [ctx_spec, lat_spec]
        out_shape = [jax.ShapeDtypeStruct((T_P, D), f32), jax.ShapeDtypeStruct((T_S, D), f32)]
    else:
        out_specs = one_spec
        out_shape = jax.ShapeDtypeStruct((T, D), f32)
    return pl.pallas_call(
        functools.partial(_ffn_kernel, final_norm=final_norm, n_ctx=n_ctx, split_in=split_in, split_out=split_out),
        grid=(T // tm, nf),
        in_specs=([ctx_spec, lat_spec] if split_in else [one_spec]) + [
            vec, mod, mod, mod,
            pl.BlockSpec((None, None, D, FFN_TF), lambda i, f: (layer, half, 0, f)),
            pl.BlockSpec((None, None, D, FFN_TF), lambda i, f: (layer, half, 0, f + nf)),
            pl.BlockSpec((None, None, FFN_TF, D), lambda i, f: (layer, half, f, 0)),
            vec],
        out_specs=out_specs,
        out_shape=out_shape,
        scratch_shapes=[pltpu.VMEM((tm, D), bf16), pltpu.VMEM((tm, D), f32)],
        compiler_params=_cparams(("arbitrary" if split_out else "parallel", "arbitrary")),
        name="ffn_final" if split_out else "ffn",
    )(*(x if split_in else (x,)), nw.reshape(1, D), sh, sc, gate, wgu, wgu, wd, final_w.reshape(1, D))


LOG2E = math.log2(math.e)


def _exp2_weights(segs):
    m = functools.reduce(jnp.maximum, [jnp.max(s, axis=-1, keepdims=True) for s in segs])
    es = [jnp.exp2(s - m) for s in segs]
    l = functools.reduce(jnp.add, [jnp.sum(e, axis=-1, keepdims=True) for e in es])
    return es, 1.0 / l


def _diff_attn_kernel(*refs, scale, lam_init, has_cache, tq):
    if has_cache:
        lam_ref, sub_ref, q_ref, k_ref, v_ref, kc_ref, vc_ref, o_ref, kb_ref, vb_ref = refs
    else:
        lam_ref, sub_ref, q_ref, k_ref, v_ref, o_ref, kb_ref, vb_ref = refs
    seq = k_ref.shape[0]
    kb_ref[0:seq, :] = k_ref[...].astype(bf16)
    vb_ref[0:seq, :] = v_ref[...].astype(bf16)
    if has_cache:
        kb_ref[seq:seq + PAST, :] = kc_ref[0].astype(bf16)
        vb_ref[seq:seq + PAST, :] = vc_ref[0].astype(bf16)
    lp = lam_ref[...]
    lam = (jnp.exp(jnp.sum(lp[0:1] * lp[1:2], axis=-1, keepdims=True))
           - jnp.exp(jnp.sum(lp[2:3] * lp[3:4], axis=-1, keepdims=True)) + lam_init)
    sub = sub_ref[...]

    def tile(rows):
        q = q_ref[rows, :] * (scale * LOG2E)
        maps = []
        for m in range(2):
            sl = slice(m * DIFF_HD, (m + 1) * DIFF_HD)
            (e,), inv = _exp2_weights([_dot_nt(q[:, sl].astype(bf16), kb_ref[:, sl])])
            maps.append(_dot(e.astype(bf16), vb_ref[...]) * inv)
        o = maps[0] - lam * maps[1]
        o = o * lax.rsqrt(jnp.mean(o * o, axis=-1, keepdims=True) + EPS) * sub
        o_ref[rows, :] = o * (1.0 - lam_init)

    _for_row_chunks(seq, tile, chunk=tq)


def _diff_attention(qkv, lam_p, subln, cache_k, cache_v, *, lam_init, latent, tq=DIFF_TQ):
    hw = 2 * DIFF_HD
    scale = DIFF_HD ** -0.5
    if latent:
        nb, seq, row0 = BS, SS, T_P
    else:
        nb, seq, row0 = BP, SP, 0
    rb = row0 // seq
    nk = seq + (PAST if latent else 0)
    specs = [pl.BlockSpec((4, DIFF_HD), lambda b, h: (0, 0)), pl.BlockSpec((1, hw), lambda b, h: (0, 0)),
             pl.BlockSpec((seq, hw), lambda b, h: (rb + b, h)),
             pl.BlockSpec((seq, hw), lambda b, h: (rb + b, DIFF_HEADS + h)),
             pl.BlockSpec((seq, hw), lambda b, h: (rb + b, 2 * DIFF_HEADS + h))]
    ins = [lam_p, subln.reshape(1, hw), qkv, qkv, qkv]
    if latent:
        specs += [pl.BlockSpec((1, PAST, hw), lambda b, h: (b, 0, h))] * 2
        ins += [cache_k, cache_v]
    return pl.pallas_call(
        functools.partial(_diff_attn_kernel, scale=scale, lam_init=lam_init, has_cache=latent, tq=min(tq, seq)),
        grid=(nb, DIFF_HEADS), in_specs=specs,
        out_specs=pl.BlockSpec((seq, hw), lambda b, h: (b, h)),
        out_shape=jax.ShapeDtypeStruct((nb * seq, D), f32),
        scratch_shapes=[pltpu.VMEM((nk, hw), bf16), pltpu.VMEM((nk, hw), bf16)],
        compiler_params=_cparams(("parallel", "parallel")),
        name="diff_attn_latent" if latent else "diff_attn_context")(*ins)


def _attn_kernel(q_ref, k_ref, v_ref, o_ref, *, scale):
    q = (q_ref[...] * (scale * LOG2E)).astype(bf16)
    (e,), inv = _exp2_weights([_dot_nt(q, k_ref[...].astype(bf16))])
    o_ref[...] = _dot(e.astype(bf16), v_ref[...].astype(bf16)) * inv


def _context_attention(qkv, *, heads, hd, scale):
    return pl.pallas_call(
        functools.partial(_attn_kernel, scale=scale),
        grid=(BP, heads),
        in_specs=[pl.BlockSpec((SP, hd), lambda b, h: (b, h)),
                  pl.BlockSpec((SP, hd), lambda b, h: (b, heads + h)),
                  pl.BlockSpec((SP, hd), lambda b, h: (b, 2 * heads + h))],
        out_specs=pl.BlockSpec((SP, hd), lambda b, h: (b, h)),
        out_shape=jax.ShapeDtypeStruct((T_P, heads * hd), f32),
        compiler_params=_cparams(("parallel", "parallel")),
        name="context_attn")(qkv, qkv, qkv)


def _mla_attn_kernel(*refs, scale, has_cache, tq):
    if has_cache:
        q_ref, kn_ref, kr_ref, v_ref, knc_ref, krc_ref, vc_ref, o_ref, kb_ref, vb_ref = refs
    else:
        q_ref, kn_ref, kr_ref, v_ref, o_ref, kb_ref, vb_ref = refs
    seq = kn_ref.shape[0]
    kb_ref[0:seq, 0:MLA_NOPE] = kn_ref[...].astype(bf16)
    kb_ref[0:seq, MLA_NOPE:] = kr_ref[...].astype(bf16)
    vb_ref[0:seq, :] = v_ref[...].astype(bf16)
    if has_cache:
        kb_ref[seq:seq + PAST, 0:MLA_NOPE] = knc_ref[...].astype(bf16)
        kb_ref[seq:seq + PAST, MLA_NOPE:] = krc_ref[...].astype(bf16)
        vb_ref[seq:seq + PAST, :] = vc_ref[...].astype(bf16)

    def tile(rows):
        q = (q_ref[rows, :] * (scale * LOG2E)).astype(bf16)
        (e,), inv = _exp2_weights([_dot_nt(q, kb_ref[...])])
        o_ref[rows, :] = _dot(e.astype(bf16), vb_ref[...]) * inv

    _for_row_chunks(seq, tile, chunk=tq)


def _mla_attention(q, kvx, lat, kvc, krc, *, latent, tq=MLA_TQ):
    scale = (MLA_NOPE + MLA_ROPE) ** -0.5
    kr_blk = (MLA_Q_LORA + MLA_KV_LORA) // V7X_LANES
    if latent:
        nb, seq, row0 = BS, SS, T_P
    else:
        nb, seq, row0 = BP, SP, 0
    rb = row0 // seq
    nk = seq + (PAST if latent else 0)
    specs = [pl.BlockSpec((seq, MLA_QW), lambda b, h: (rb + b, h)),
             pl.BlockSpec((seq, MLA_NOPE), lambda b, h: (rb + b, 2 * h)),
             pl.BlockSpec((seq, V7X_LANES), lambda b, h: (rb + b, kr_blk)),
             pl.BlockSpec((seq, MLA_V), lambda b, h: (rb + b, 2 * h + 1))]
    ins = [q, kvx, lat, kvx]
    if latent:
        specs += [pl.BlockSpec((PAST, MLA_NOPE), lambda b, h: (b, 2 * h)),
                  pl.BlockSpec((PAST, V7X_LANES), lambda b, h: (b, 0)),
                  pl.BlockSpec((PAST, MLA_V), lambda b, h: (b, 2 * h + 1))]
        ins += [kvc, krc, kvc]
    return pl.pallas_call(
        functools.partial(_mla_attn_kernel, scale=scale, has_cache=latent, tq=min(tq, seq)),
        grid=(nb, MLA_HEADS), in_specs=specs,
        out_specs=pl.BlockSpec((seq, MLA_V), lambda b, h: (b, h)),
        out_shape=jax.ShapeDtypeStruct((nb * seq, MLA_HEADS * MLA_V), f32),
        scratch_shapes=[pltpu.VMEM((nk, MLA_QW), bf16), pltpu.VMEM((nk, MLA_V), bf16)],
        compiler_params=_cparams(("parallel", "parallel")),
        name="mla_attn_latent" if latent else "mla_attn_context")(*ins)


def _na_block_plan():
    plan = []
    for r0 in range(0, GRID_R, NA_QROWS):
        kb = min(max(r0 - NA_WIN_ROWS // 2, 0), GRID_R - NA_KROWS)
        var = 0 if r0 == 0 else (2 if r0 == GRID_R - NA_QROWS else 1)
        plan.append((r0, kb, var))
    return plan


def _na_bias_tables(rpb):
    reps = {var: (r0, kb) for r0, kb, var in _na_block_plan()}
    nr, ncol = 2 * NA_WIN_ROWS - 1, 2 * NA_WIN_COLS - 1
    rsel = np.zeros((3, NA_QROWS, NA_KROWS, nr), np.float32)
    for var in range(3):
        r0, kb = reps[var]
        for i in range(NA_QROWS):
            r = r0 + i
            rs = min(max(r - NA_WIN_ROWS // 2, 0), GRID_R - NA_WIN_ROWS)
            for kk in range(NA_KROWS):
                kr = kb + kk
                if rs <= kr < rs + NA_WIN_ROWS:
                    rsel[var, i, kk, kr - r + NA_WIN_ROWS - 1] = 1.0
    csel = np.zeros((GRID_W, GRID_W, ncol), np.float32)
    for c in range(GRID_W):
        cstart = min(max(c - NA_WIN_COLS // 2, 0), GRID_W - NA_WIN_COLS)
        for kc in range(cstart, cstart + NA_WIN_COLS):
            csel[c, kc, kc - c + NA_WIN_COLS - 1] = 1.0
    ok = (rsel.sum(-1) > 0)[:, None, :, None, :, None] & (csel.sum(-1) > 0)[None, None, None, :, None, :]
    val = jnp.einsum("vikr,hrd->vhikd", rsel, rpb, precision=lax.Precision.HIGHEST)
    tab = jnp.einsum("vhikd,cjd->vhickj", val, csel, precision=lax.Precision.HIGHEST)
    tab = jnp.where(ok, tab * LOG2E, -1e30)
    return tab.reshape(3, NA_HEADS, NA_QROWS * GRID_W, NA_KROWS * GRID_W).astype(f32)


def _na_kernel(q_ref, k_ref, v_ref, ck_ref, cv_ref, bias_ref, o_ref, *, scale):
    ck = ck_ref[0].astype(bf16)
    cv = cv_ref[0].astype(bf16)
    nq = NA_QROWS * GRID_W
    nk = NA_KROWS * GRID_W
    for r0, kb, var in _na_block_plan():
        q = (q_ref[r0 * GRID_W:r0 * GRID_W + nq, :] * (scale * LOG2E)).astype(bf16)
        k = k_ref[kb * GRID_W:kb * GRID_W + nk, :].astype(bf16)
        v = v_ref[kb * GRID_W:kb * GRID_W + nk, :].astype(bf16)
        (e_loc, e_ctx), inv = _exp2_weights([_dot_nt(q, k) + bias_ref[var, 0], _dot_nt(q, ck)])
        o_ref[r0 * GRID_W:r0 * GRID_W + nq, :] = (_dot(e_loc.astype(bf16), v) + _dot(e_ctx.astype(bf16), cv)) * inv


def _na_latent_attention(qkv, cache_k, cache_v, bias):
    scale = NA_HD ** -0.5
    rb = T_P // SS
    nk = NA_KROWS * GRID_W
    return pl.pallas_call(
        functools.partial(_na_kernel, scale=scale),
        grid=(NA_HEADS, BS),
        in_specs=[pl.BlockSpec((SS, NA_HD), lambda h, b: (rb + b, h)),
                  pl.BlockSpec((SS, NA_HD), lambda h, b: (rb + b, NA_HEADS + h)),
                  pl.BlockSpec((SS, NA_HD), lambda h, b: (rb + b, 2 * NA_HEADS + h)),
                  pl.BlockSpec((1, PAST, NA_HD), lambda h, b: (b, 0, h)),
                  pl.BlockSpec((1, PAST, NA_HD), lambda h, b: (b, 0, h)),
                  pl.BlockSpec((3, 1, NA_QROWS * GRID_W, nk), lambda h, b: (0, h, 0, 0))],
        out_specs=pl.BlockSpec((SS, NA_HD), lambda h, b: (b, h)),
        out_shape=jax.ShapeDtypeStruct((T_S, D), f32),
        compiler_params=_cparams(("parallel", "arbitrary")),
        name="na_attn_latent",
    )(qkv, qkv, qkv, cache_k, cache_v, bias)


SSD_PAIRS = SSD_HPG // 2
SSD_DLANES = 16


def _ssd_kernel(*refs, seq, has_h0, want_state):
    refs = list(refs)
    (x_ref, b_ref, c_ref, dt_ref, cwx_ref, cwb_ref, cwc_ref, cbx_ref, cbb_ref, cbc_ref,
     alog_ref, dtb_ref, dsk_ref) = refs[:13]
    refs = refs[13:]
    h0_ref = refs.pop(0) if has_h0 else None
    y_ref = refs.pop(0)
    hT_ref = refs.pop(0) if want_state else None
    xp_ref, bp_ref, cp_ref, xs_ref, bt_ref, cs_ref, ac_ref, act_ref, dtt_ref, h_ref = refs
    nc = seq // SSD_Q
    halo = 8

    for src, pad, dst, cw_ref, cb_ref, transpose in ((x_ref, xp_ref, xs_ref, cwx_ref, cbx_ref, False),
                                                     (b_ref, bp_ref, bt_ref, cwb_ref, cbb_ref, True),
                                                     (c_ref, cp_ref, cs_ref, cwc_ref, cbc_ref, False)):
        width = src.shape[1]
        pad[0:halo, :] = jnp.zeros((halo, width), f32)
        pad[halo + seq:2 * halo + seq, :] = jnp.zeros((halo, width), f32)
        pad[halo:halo + seq, :] = src[...]
        cw = cw_ref[...]
        cb = cb_ref[...]

        def conv_chunk(c, carry, pad=pad, dst=dst, cw=cw, cb=cb, width=width, transpose=transpose):
            base = pl.multiple_of(c * SSD_Q, SSD_Q)
            win = pad[pl.ds(base, SSD_Q + 2 * halo), :]
            acc = jnp.broadcast_to(cb, (SSD_Q, width))
            for t in range(SSD_CONV_W):
                off = halo - SSD_CONV_W // 2 + t
                sh = pltpu.roll(win, SSD_Q + 2 * halo - off, axis=0)[:SSD_Q]
                acc = acc + sh * cw[t:t + 1, :]
            out = _silu(acc)
            dst[pl.ds(base, SSD_Q), :] = out.T if transpose else out
            return carry

        lax.fori_loop(0, nc, conv_chunk, 0)

    row = lax.broadcasted_iota(jnp.int32, (SSD_Q, SSD_Q), 0)
    col = lax.broadcasted_iota(jnp.int32, (SSD_Q, SSD_Q), 1)
    tri = {0: (col <= row), 1: (col >= row)}
    low_half = col < SSD_P
    a_neg = -jnp.exp(alog_ref[0])
    dtb = dtb_ref[0]
    dsk = dsk_ref[...]
    tril = tri[0].astype(f32)

    def dt_chunk(c, carry):
        base = pl.multiple_of(c * SSD_Q, SSD_Q)
        dt = dt_ref[pl.ds(base, SSD_Q), :] + dtb
        dt = jnp.maximum(dt, 0.0) + jnp.log1p(jnp.exp(-jnp.abs(dt)))
        a = dt * a_neg
        pre = _dot_onehot_exact(tril, a)
        suf = pre[SSD_Q - 1:SSD_Q, :] - pre + a
        acum = jnp.where(col < SSD_HPG, pre, suf)
        ac_ref[pl.ds(base, SSD_Q), :] = acum
        tb = pl.multiple_of(c * SSD_DLANES, SSD_DLANES)
        act_ref[pl.ds(tb, SSD_DLANES), :] = acum.T[:SSD_DLANES]
        dtt_ref[pl.ds(tb, SSD_DLANES), :] = dt.T[:SSD_DLANES]
        return carry

    lax.fori_loop(0, nc, dt_chunk, 0)

    for direction in range(2):
        mask = tri[direction]
        o = direction * SSD_HPG
        for k in range(SSD_PAIRS):
            if has_h0:
                h_ref[k] = h0_ref[0, direction, k * SSD_Q:(k + 1) * SSD_Q, :].T
            else:
                h_ref[k] = jnp.zeros((SSD_N, 2 * SSD_P), f32)

        def chunk(ci, carry, direction=direction, mask=mask, o=o):
            c = ci if direction == 0 else nc - 1 - ci
            base = pl.multiple_of(c * SSD_Q, SSD_Q)
            rows = pl.ds(base, SSD_Q)
            tb = pl.multiple_of(c * SSD_DLANES, SSD_DLANES)
            xc = xs_ref[rows, :]
            cc = cs_ref[rows, :]
            bt = bt_ref[rows, :]
            acum = ac_ref[rows, :]
            acum_t = act_ref[pl.ds(tb, SSD_DLANES), :]
            dt_t = dtt_ref[pl.ds(tb, SSD_DLANES), :]
            tot = acum_t[:, SSD_Q - 1:SSD_Q] if direction == 0 else acum_t[:, 0:1]
            w_t = dt_t * jnp.exp(tot - acum_t)
            decay = jnp.exp(tot)
            cb = _dot(cc.astype(bf16), bt.astype(bf16))
            ys = []
            for k in range(SSD_PAIRS):
                xk = xc[:, k * 2 * SSD_P:(k + 1) * 2 * SSD_P].astype(bf16)
                hk = h_ref[k]
                y_pair = None
                st_pair = None
                for e in range(2):
                    j = o + 2 * k + e
                    half = low_half if e == 0 else jnp.logical_not(low_half)
                    a_q = jnp.broadcast_to(acum[:, j:j + 1], (SSD_Q, SSD_Q))
                    seg = a_q - acum_t[j:j + 1, :]
                    lmat = jnp.where(mask, jnp.exp(jnp.where(mask, seg, 0.0)), 0.0)
                    lhs = jnp.concatenate([cb * lmat * dt_t[j:j + 1, :], cc * jnp.exp(a_q)], axis=1).astype(bf16)
                    x_e = jnp.where(half, xk, jnp.zeros_like(xk))
                    h_e = jnp.where(half, hk, 0.0).astype(bf16)
                    part = _dot(lhs, jnp.concatenate([x_e, h_e], axis=0))
                    st = _dot((bt * w_t[j:j + 1, :]).astype(bf16), x_e)
                    y_pair = part if y_pair is None else y_pair + part
                    st_pair = st if st_pair is None else st_pair + st
                d_row = jnp.where(low_half[0:1], decay[o + 2 * k:o + 2 * k + 1, :], decay[o + 2 * k + 1:o + 2 * k + 2, :])
                h_ref[k] = hk * d_row + st_pair
                ys.append(y_pair)
            y = jnp.concatenate(ys, axis=1)
            if direction == 0:
                y_ref[rows, :] = y + xc * dsk
            else:
                y_ref[rows, :] += y
            return carry

        lax.fori_loop(0, nc, chunk, 0)
        if want_state:
            for k in range(SSD_PAIRS):
                hT_ref[0, direction, k * SSD_Q:(k + 1) * SSD_Q, :] = h_ref[k].T


def _ssd_group_dt_columns(m):
    rows = m.shape[0]
    fb = m.reshape(rows, 2, SSD_G, SSD_HPG).transpose(0, 2, 1, 3).reshape(rows, SSD_G, SSD_DLANES)
    return jnp.pad(fb, ((0, 0), (0, 0), (0, V7X_LANES - SSD_DLANES))).reshape(rows, SSD_G * V7X_LANES)


def _ssd_scan(proj, conv_w, conv_b, a_log, dt_bias, d_skip, h0, *, latent):
    if latent:
        nb, seq, row0 = BS, SS, T_P
    else:
        nb, seq, row0 = BP, SP, 0
    rb = row0 // seq
    xb0 = SSD_DI // SSD_GW
    bb0 = 2 * SSD_DI // SSD_N
    cb0 = bb0 + SSD_G
    dtb0 = (SSD_DI + SSD_CONV_CH) // V7X_LANES
    cwb0 = SSD_DI // SSD_N
    row = lambda blk: (lambda b, g: (rb + b, blk(g)))
    grp = pl.BlockSpec((1, 1, V7X_LANES), lambda b, g: (g, 0, 0))
    specs = [pl.BlockSpec((seq, SSD_GW), row(lambda g: xb0 + g)),
             pl.BlockSpec((seq, SSD_N), row(lambda g: bb0 + g)),
             pl.BlockSpec((seq, SSD_N), row(lambda g: cb0 + g)),
             pl.BlockSpec((seq, V7X_LANES), row(lambda g: dtb0 + g)),
             pl.BlockSpec((SSD_CONV_W, SSD_GW), lambda b, g: (0, g)),
             pl.BlockSpec((SSD_CONV_W, SSD_N), lambda b, g: (0, cwb0 + g)),
             pl.BlockSpec((SSD_CONV_W, SSD_N), lambda b, g: (0, cwb0 + SSD_G + g)),
             pl.BlockSpec((1, SSD_GW), lambda b, g: (0, g)),
             pl.BlockSpec((1, SSD_N), lambda b, g: (0, cwb0 + g)),
             pl.BlockSpec((1, SSD_N), lambda b, g: (0, cwb0 + SSD_G + g)),
             grp, grp,
             pl.BlockSpec((1, SSD_GW), lambda b, g: (0, g))]
    cbr = conv_b.reshape(1, SSD_CONV_CH)
    ins = [proj, proj, proj, proj, conv_w, conv_w, conv_w, cbr, cbr, cbr,
           _ssd_group_dt_columns(a_log.reshape(1, 2 * SSD_HEADS)).reshape(SSD_G, 1, V7X_LANES),
           _ssd_group_dt_columns(dt_bias.reshape(1, 2 * SSD_HEADS)).reshape(SSD_G, 1, V7X_LANES),
           jnp.repeat(d_skip, SSD_P).reshape(1, SSD_DI)]
    state_spec = pl.BlockSpec((1, 2, SSD_GW, SSD_N), lambda b, g: (b, 0, g, 0))
    if latent:
        specs.append(state_spec)
        ins.append(h0)
    out_shape = [jax.ShapeDtypeStruct((nb * seq, SSD_DI), f32)]
    out_specs = [pl.BlockSpec((seq, SSD_GW), lambda b, g: (b, g))]
    if not latent:
        out_shape.append(jax.ShapeDtypeStruct((nb, 2, SSD_HEADS * SSD_P, SSD_N), f32))
        out_specs.append(state_spec)
    pad_rows = seq + 16
    return pl.pallas_call(
        functools.partial(_ssd_kernel, seq=seq, has_h0=latent, want_state=not latent),
        grid=(nb, SSD_G), in_specs=specs, out_specs=out_specs, out_shape=out_shape,
        scratch_shapes=[pltpu.VMEM((pad_rows, SSD_GW), f32), pltpu.VMEM((pad_rows, SSD_N), f32),
                        pltpu.VMEM((pad_rows, SSD_N), f32),
                        pltpu.VMEM((seq, SSD_GW), f32), pltpu.VMEM((seq, SSD_Q), f32), pltpu.VMEM((seq, SSD_N), f32),
                        pltpu.VMEM((seq, V7X_LANES), f32),
                        pltpu.VMEM((seq // SSD_Q * SSD_DLANES, SSD_Q), f32),
                        pltpu.VMEM((seq // SSD_Q * SSD_DLANES, SSD_Q), f32),
                        pltpu.VMEM((SSD_PAIRS, SSD_N, 2 * SSD_P), f32)],
        compiler_params=_cparams(("parallel", "arbitrary")),
        name="ssd_latent" if latent else "ssd_context")(*ins)


def _axial_tables(half):
    t = np.arange(SS)
    n = half // 2
    inv = ROPE_BASE ** (-np.arange(n, dtype=np.float32) / n)
    cos, sin = [], []
    for pos in (t // GRID_W, t % GRID_W):
        ang = pos.astype(np.float32)[:, None] * inv[None].astype(np.float32)
        cos += [np.cos(ang), np.cos(ang)]
        sin += [-np.sin(ang), np.sin(ang)]
    return np.concatenate(cos, axis=1).astype(np.float32), np.concatenate(sin, axis=1).astype(np.float32)


def _token_tables(cos_lat, sin_lat):
    w = cos_lat.shape[1]
    cos = np.concatenate([np.ones((T_P, w), np.float32), np.tile(cos_lat, (BS, 1))], axis=0)
    sin = np.concatenate([np.zeros((T_P, w), np.float32), np.tile(sin_lat, (BS, 1))], axis=0)
    return jnp.asarray(cos), jnp.asarray(sin)


def _rope_tables():
    c128, s128 = _axial_tables(DIFF_HD // 2)
    c64, s64 = _axial_tables(MLA_ROPE // 2)
    one, zero = np.ones((SS, 64), np.float32), np.zeros((SS, 64), np.float32)
    one2, zero2 = np.ones((SS, 128), np.float32), np.zeros((SS, 128), np.float32)
    diff = _token_tables(c128, s128)
    mla_q = _token_tables(np.concatenate([one2, c64, one], 1), np.concatenate([zero2, s64, zero], 1))
    mla_kr = _token_tables(np.concatenate([c64, one], 1), np.concatenate([s64, zero], 1))
    return diff, mla_q, mla_kr


def kernel(x_prompt, x_sample, cache_diff_k, cache_diff_v, cache_na_k, cache_na_v, cache_mla_ckv, cache_mla_krope,
           state_ssd, c, c_ctx, ada_w, ada_b, norm_w, ffn_w_gu, ffn_w_down, final_norm_w, diff_w_qkv, diff_w_o,
           diff_lambda, diff_subln_w, na_w_qkv, na_w_o, na_rpb, mla_w_dq, mla_q_norm, mla_w_uq, mla_w_dkv,
           mla_kv_norm, mla_w_ukv, mla_w_o, ssd_w_in, ssd_conv_w, ssd_conv_b, ssd_a_log, ssd_dt_bias, ssd_d,
           ssd_norm_w, ssd_w_out):
    x = (x_prompt.reshape(T_P, D), x_sample.reshape(T_S, D))
    wgu, wd = _ffn_weights(ffn_w_gu, ffn_w_down)
    cond =jnp.concatenate([c_ctx[None], c, jnp.zeros((COND_PAD - N_COND, D), f32)], axis=0)
    mods = _modulation(cond, ada_w, ada_b)[:, :N_COND].reshape(DEPTH, N_COND, 3, 3, 1, D)
    rope_diff, rope_mla_q, rope_mla_kr = _rope_tables()

    def mod(i, k, what):
        return mods[i, :, k, what]

    def o_proj(a, w_o, i, **kw):
        return _fused_matmul(a, w_o.astype(bf16), epi="residual", epi_args=(x, mod(i, 1, 2)), tn=D, **kw)

    outs = {}
    for i in range(DEPTH):
        x = _ffn(x, norm_w[i, 0], mod(i, 0, 0), mod(i, 0, 1), mod(i, 0, 2), wgu, wd, i, 0, final_norm_w,
                 final_norm=False)
        pre_args = (norm_w[i, 1], mod(i, 1, 0), mod(i, 1, 1))
        kind = i % 4
        if kind == 0:
            lam_init = 0.8 - 0.6 * math.exp(-0.3 * i)
            qkv = _fused_matmul(x, diff_w_qkv[0].astype(bf16), pre="mod", pre_args=pre_args, epi="rope",
                                epi_args=rope_diff, rope=(2 * D // QKV_TN, 0, QKV_TN, DIFF_HD // 4),
                                tm=QKV_TM, tn=QKV_TN, name="diff_qkv")
            outs["diff_k"] = qkv[:T_P, D:2 * D].reshape(BP, 1, SP, DIFF_HEADS, 2, DIFF_HD)
            outs["diff_v"] = qkv[:T_P, 2 * D:].reshape(BP, 1, SP, DIFF_HEADS, 2 * DIFF_HD)
            o_ctx = _diff_attention(qkv, diff_lambda[0], diff_subln_w[0], None, None, lam_init=lam_init, latent=False)
            o_lat = _diff_attention(qkv, diff_lambda[0], diff_subln_w[0], cache_diff_k.reshape(BS, PAST, D),
                                    cache_diff_v.reshape(BS, PAST, D), lam_init=lam_init, latent=True)
            x = o_proj((o_ctx, o_lat), diff_w_o[0], i, name="diff_out")
        elif kind == 1:
            qkv = _fused_matmul(x, na_w_qkv[0].astype(bf16), pre="mod", pre_args=pre_args, tm=QKV_TM, tn=QKV_TN,
                                name="na_qkv")
            outs["na_k"] = qkv[:T_P, D:2 * D].reshape(BP, 1, SP, NA_HEADS, NA_HD)
            outs["na_v"] = qkv[:T_P, 2 * D:].reshape(BP, 1, SP, NA_HEADS, NA_HD)
            o_ctx = _context_attention(qkv, heads=NA_HEADS, hd=NA_HD, scale=NA_HD ** -0.5)
            o_lat = _na_latent_attention(qkv, cache_na_k.reshape(BS, PAST, D), cache_na_v.reshape(BS, PAST, D),
                                         _na_bias_tables(na_rpb[0]))
            x = o_proj((o_ctx, o_lat), na_w_o[0], i, name="na_out")
        elif kind == 2:
            w_lat = jnp.concatenate([mla_w_dq[0], mla_w_dkv[0],
                                     jnp.zeros((D, V7X_LANES - MLA_ROPE), f32)], axis=1).astype(bf16)
            n_lat = w_lat.shape[1]
            lat = _fused_matmul(x, w_lat, pre="mod", pre_args=pre_args, epi="rope", epi_args=rope_mla_kr,
                                rope=(1, n_lat - V7X_LANES, n_lat, MLA_ROPE // 4), tn=n_lat, name="mla_down")
            w_uq = jnp.pad(mla_w_uq[0].reshape(MLA_Q_LORA, MLA_HEADS, MLA_NOPE + MLA_ROPE),
                           ((0, 0), (0, 0), (0, MLA_QW - MLA_NOPE - MLA_ROPE))).reshape(MLA_Q_LORA, MLA_HEADS * MLA_QW)
            q = _fused_matmul(lat, w_uq.astype(bf16), pre="rms", pre_args=(mla_q_norm[0],), epi="rope",
                              epi_args=rope_mla_q, rope=(MLA_HEADS * MLA_QW // QKV_TN, 0, QKV_TN, MLA_ROPE // 4),
                              tm=QKV_TM, tn=QKV_TN, name="mla_q")
            w_ukv = mla_w_ukv[0].astype(bf16)
            kvx, ckv_n = _fused_matmul(lat, w_ukv, k_blk=MLA_Q_LORA // MLA_KV_LORA, pre="rms",
                                       pre_args=(mla_kv_norm[0],), want_pre_out=True, tm=QKV_TM, tn=QKV_TN,
                                       name="mla_kv")
            kvc = _fused_matmul(cache_mla_ckv.reshape(BS * PAST, MLA_KV_LORA), w_ukv, tm=QKV_TM, tn=QKV_TN,
                                name="mla_kv_cache")
            krc = jnp.pad(cache_mla_krope.reshape(BS * PAST, MLA_ROPE), ((0, 0), (0, V7X_LANES - MLA_ROPE)))
            outs["mla_ckv"] = ckv_n[:T_P].reshape(BP, 1, SP, MLA_KV_LORA)
            outs["mla_kr"] = lat[:T_P, MLA_Q_LORA + MLA_KV_LORA:MLA_Q_LORA + MLA_KV_LORA + MLA_ROPE].reshape(
                BP, 1, SP, MLA_ROPE)
            o_ctx = _mla_attention(q, kvx, lat, None, None, latent=False)
            o_lat = _mla_attention(q, kvx, lat, kvc, krc, latent=True)
            x = o_proj((o_ctx, o_lat), mla_w_o[0], i, name="mla_out")
        else:
            n_xz = SSD_DI + SSD_CONV_CH
            w_in = jnp.concatenate([ssd_w_in[0][:, :n_xz], _ssd_group_dt_columns(ssd_w_in[0][:, n_xz:])], axis=1)
            proj = _fused_matmul(x, w_in.astype(bf16), pre="mod", pre_args=pre_args, tm=QKV_TM, tn=QKV_TN,
                                 name="ssd_in")
            args = (proj, ssd_conv_w[0], ssd_conv_b[0], ssd_a_log[0], ssd_dt_bias[0], ssd_d[0])
            y_ctx, h_t = _ssd_scan(*args, None, latent=False)
            (y_lat,) = _ssd_scan(*args, state_ssd.reshape(BS, 2, SSD_HEADS * SSD_P, SSD_N), latent=True)
            outs["ssd_h"] = h_t.reshape(BP, 1, 2, SSD_HEADS, SSD_P, SSD_N)
            x = o_proj((y_ctx, y_lat), ssd_w_out[0], i, pre="gated_rms", pre_args=(proj, ssd_norm_w[0]), tm=TM // 2,
                       name="ssd_out")
        last = i == DEPTH - 1
        x = _ffn(x, norm_w[i, 2], mod(i, 2, 0), mod(i, 2, 1), mod(i, 2, 2), wgu, wd, i, 1, final_norm_w,
                 final_norm=last, split_out=last)
    y_ctx, y_lat = x
    return (y_ctx.reshape(BP, SP, D), y_lat.reshape(BS, SS, D), outs["diff_k"], outs["diff_v"], outs["na_k"],
            outs["na_v"], outs["mla_ckv"], outs["mla_kr"], outs["ssd_h"])
```
